```python
import math
import jax, jax.numpy as jnp
from jax import lax
import numpy as np

D_MODEL = 1024
BATCH = 8
SEQ = 4096
DEPTH = 1
DEC_BATCH = 8
DEC_SEQ = 2048
PAST_LEN = 128

HEAD_DIM = 64
A_HEADS = 8
A_WIDTH = A_HEADS * HEAD_DIM
B_HEADS = 4
B_QK_DIM = 64
B_V_DIM = 2 * B_QK_DIM
B_WIDTH = B_HEADS * B_V_DIM
MIX_WIDTH = A_WIDTH + B_WIDTH
IN_COLS = 3 * A_WIDTH + 3 * B_WIDTH
D_FF = 2816
ROPE_THETA = 10000.0
EPS = 1e-6
NEG_INF = -1e30
DILATED_PATTERNS = ((128, 1), (512, 4), (2048, 16))
LOCAL_BLOCK = 64
Q_BLOCK_DENSE = 128

kernel_name = "hymba_dilated_diff_macaron_encoder"


def rms_norm(x, g):
    xf = x.astype(jnp.float32)
    y = xf * lax.rsqrt(jnp.mean(xf * xf, axis=-1, keepdims=True) + EPS)
    return (y * g.astype(jnp.float32)).astype(x.dtype)


def rope_tables(seq, dim):
    inv = 1.0 / (ROPE_THETA ** (jnp.arange(0, dim, 2, dtype=jnp.float32) / dim))
    ang = jnp.arange(seq, dtype=jnp.float32)[:, None] * inv[None, :]
    ang = jnp.concatenate([ang, ang], axis=-1)
    return jnp.cos(ang), jnp.sin(ang)


def apply_rope(x, cos, sin):
    shape = (x.shape[1],) + (1,) * (x.ndim - 3) + (x.shape[-1],)
    c = cos.reshape(shape)
    s = sin.reshape(shape)
    x1, x2 = jnp.split(x, 2, axis=-1)
    rot = jnp.concatenate([-x2, x1], axis=-1)
    return (x * c + rot * s).astype(x.dtype)


def swiglu(x, w_gu, w_down):
    g, u = jnp.split(x @ w_gu, 2, axis=-1)
    return (jax.nn.silu(g) * u) @ w_down


def dilated_window_attention(q, k, v, dilation, half):
    B, S, H, D = q.shape
    L = S // dilation
    qs = q.reshape(B, L, dilation, H, D)
    ks = k.reshape(B, L, dilation, H, D)
    vs = v.reshape(B, L, dilation, H, D)
    bq = math.gcd(L, LOCAL_BLOCK)
    nb = L // bq
    kw = bq + 2 * half
    pad = ((0, 0), (half, half), (0, 0), (0, 0), (0, 0))
    kp = jnp.pad(ks, pad)
    vp = jnp.pad(vs, pad)
    idx = (jnp.arange(nb) * bq)[:, None] + jnp.arange(kw)[None, :]
    kb = kp[:, idx]
    vb = vp[:, idx]
    qb = qs.reshape(B, nb, bq, dilation, H, D)
    s = jnp.einsum('bnqrhd,bnkrhd->bnrhqk', qb, kb,
                   preferred_element_type=jnp.float32) * (D ** -0.5)
    rel = jnp.arange(kw)[None, :] - half - jnp.arange(bq)[:, None]
    kpos = idx - half
    valid = (jnp.abs(rel) <= half)[None] & ((kpos >= 0) & (kpos < L))[:, None, :]
    s = jnp.where(valid[None, :, None, None], s, NEG_INF)
    m = jnp.max(s, axis=-1, keepdims=True)
    p = jnp.exp(s - m)
    l = jnp.sum(p, axis=-1, keepdims=True)
    o = jnp.einsum('bnrhqk,bnkrhd->bnqrhd', p, vb.astype(jnp.float32))
    o = o / jnp.transpose(l, (0, 1, 4, 2, 3, 5))
    lse = (m + jnp.log(l))[..., 0]
    lse = jnp.transpose(lse, (0, 1, 4, 2, 3)).reshape(B, S, H)
    return o.reshape(B, S, H, D), lse


def diff_attention(q, k, v, lam):
    B, S, H, _, D = q.shape
    nblk = S // Q_BLOCK_DENSE
    qb = jnp.transpose(q.reshape(B, nblk, Q_BLOCK_DENSE, H, 2, D), (1, 0, 2, 3, 4, 5))

    def one_block(qblk):
        s = jnp.einsum('bqhcd,bkhcd->bhcqk', qblk, k,
                       preferred_element_type=jnp.float32) * (D ** -0.5)
        p = jax.nn.softmax(s, axis=-1)
        a = p[:, :, 0] - lam * p[:, :, 1]
        return jnp.einsum('bhqk,bkhe->bqhe', a.astype(v.dtype), v)

    o = lax.map(one_block, qb)
    return jnp.transpose(o, (1, 0, 2, 3, 4)).reshape(B, S, H, 2 * D)


def encoder_layer(x, layer, g_ffn1, w_ffn1_gu, w_ffn1_down, g_mix, w_in,
                  g_a_q, g_a_k, g_a_out, g_b_q, g_b_k,
                  lam_q1, lam_k1, lam_q2, lam_k2, g_b_out, w_out,
                  g_ffn2, w_ffn2_gu, w_ffn2_down, g_final):
    B, S, _ = x.shape
    cos, sin = rope_tables(S, HEAD_DIM)
    x = x + 0.5 * swiglu(rms_norm(x, g_ffn1), w_ffn1_gu, w_ffn1_down)
    h = rms_norm(x, g_mix)
    proj = h @ w_in
    a_q, a_k, a_v, b_q, b_k, b_v = jnp.split(
        proj, [A_WIDTH, 2 * A_WIDTH, 3 * A_WIDTH,
               3 * A_WIDTH + B_WIDTH, 3 * A_WIDTH + 2 * B_WIDTH], axis=-1)
    a_q = apply_rope(rms_norm(a_q.reshape(B, S, A_HEADS, HEAD_DIM), g_a_q), cos, sin)
    a_k = apply_rope(rms_norm(a_k.reshape(B, S, A_HEADS, HEAD_DIM), g_a_k), cos, sin)
    a_v = a_v.reshape(B, S, A_HEADS, HEAD_DIM)
    outs, lses = [], []
    for window, dilation in DILATED_PATTERNS:
        o_p, lse_p = dilated_window_attention(a_q, a_k, a_v, dilation, window // (2 * dilation))
        outs.append(o_p)
        lses.append(lse_p)
    wts = jax.nn.softmax(jnp.stack(lses, axis=0), axis=0)
    o_a = jnp.sum(wts[..., None] * jnp.stack(outs, axis=0), axis=0).astype(x.dtype)
    o_a = rms_norm(o_a, g_a_out)
    lambda_init = 0.8 - 0.6 * math.exp(-0.3 * layer)
    f32 = jnp.float32
    lam = (jnp.exp(jnp.dot(lam_q1.astype(f32), lam_k1.astype(f32)))
           - jnp.exp(jnp.dot(lam_q2.astype(f32), lam_k2.astype(f32))) + lambda_init)
    b_q = apply_rope(rms_norm(b_q.reshape(B, S, B_HEADS, 2, B_QK_DIM), g_b_q), cos, sin)
    b_k = apply_rope(rms_norm(b_k.reshape(B, S, B_HEADS, 2, B_QK_DIM), g_b_k), cos, sin)
    b_v = b_v.reshape(B, S, B_HEADS, B_V_DIM)
    o_b = diff_attention(b_q, b_k, b_v, lam)
    o_b = rms_norm(o_b, g_b_out) * (1.0 - lambda_init)
    mixed = jnp.concatenate([o_a.reshape(B, S, A_WIDTH),
                             o_b.reshape(B, S, B_WIDTH).astype(x.dtype)], axis=-1)
    x = x + mixed @ w_out
    x = x + 0.5 * swiglu(rms_norm(x, g_ffn2), w_ffn2_gu, w_ffn2_down)
    return rms_norm(x, g_final)


def setup_inputs(seed: int = 0) -> dict:
    key = jax.random.key(seed)
    ks = jax.random.split(key, 24)
    f32 = jnp.float32

    def nrm(k, shape, scale):
        return jax.random.normal(k, shape, f32) * scale

    def gain(k, shape):
        return 1.0 + 0.01 * jax.random.normal(k, shape, f32)

    return {
        "x_prompt": jax.random.normal(ks[0], (BATCH, SEQ, D_MODEL), f32),
        "x_sample": jax.random.normal(ks[1], (DEC_BATCH, DEC_SEQ, D_MODEL), f32),
        "g_ffn1": gain(ks[2], (DEPTH, D_MODEL)),
        "w_ffn1_gu": nrm(ks[3], (DEPTH, D_MODEL, 2 * D_FF), D_MODEL ** -0.5),
        "w_ffn1_down": nrm(ks[4], (DEPTH, D_FF, D_MODEL), D_FF ** -0.5),
        "g_mix": gain(ks[5], (DEPTH, D_MODEL)),
        "w_in": nrm(ks[6], (DEPTH, D_MODEL, IN_COLS), D_MODEL ** -0.5),
        "g_a_q": gain(ks[7], (DEPTH, HEAD_DIM)),
        "g_a_k": gain(ks[8], (DEPTH, HEAD_DIM)),
        "g_a_out": gain(ks[9], (DEPTH, A_HEADS, HEAD_DIM)),
        "g_b_q": gain(ks[10], (DEPTH, B_QK_DIM)),
        "g_b_k": gain(ks[11], (DEPTH, B_QK_DIM)),
        "lam_q1": nrm(ks[12], (DEPTH, B_QK_DIM), 0.1),
        "lam_k1": nrm(ks[13], (DEPTH, B_QK_DIM), 0.1),
        "lam_q2": nrm(ks[14], (DEPTH, B_QK_DIM), 0.1),
        "lam_k2": nrm(ks[15], (DEPTH, B_QK_DIM), 0.1),
        "g_b_out": gain(ks[16], (DEPTH, B_V_DIM)),
        "w_out": nrm(ks[17], (DEPTH, MIX_WIDTH, D_MODEL), MIX_WIDTH ** -0.5),
        "g_ffn2": gain(ks[18], (DEPTH, D_MODEL)),
        "w_ffn2_gu": nrm(ks[19], (DEPTH, D_MODEL, 2 * D_FF), D_MODEL ** -0.5),
        "w_ffn2_down": nrm(ks[20], (DEPTH, D_FF, D_MODEL), D_FF ** -0.5),
        "g_final": gain(ks[21], (DEPTH, D_MODEL)),
    }


def reference(x_prompt, x_sample, g_ffn1, w_ffn1_gu, w_ffn1_down, g_mix, w_in,
              g_a_q, g_a_k, g_a_out, g_b_q, g_b_k, lam_q1, lam_k1, lam_q2, lam_k2,
              g_b_out, w_out, g_ffn2, w_ffn2_gu, w_ffn2_down, g_final):
    def trunk(x):
        for l in range(DEPTH):
            x = encoder_layer(x, l, g_ffn1[l], w_ffn1_gu[l], w_ffn1_down[l], g_mix[l], w_in[l],
                              g_a_q[l], g_a_k[l], g_a_out[l], g_b_q[l], g_b_k[l],
                              lam_q1[l], lam_k1[l], lam_q2[l], lam_k2[l], g_b_out[l], w_out[l],
                              g_ffn2[l], w_ffn2_gu[l], w_ffn2_down[l], g_final[l])
        return x

    y_prompt = trunk(x_prompt)
    y_sample = trunk(x_sample)
    return (y_prompt, y_sample)
```

```python
import functools
import math

import jax
import jax.numpy as jnp
from jax import lax
from jax.experimental import pallas as pl
from jax.experimental.pallas import tpu as pltpu

F32 = jnp.float32
BF16 = jnp.bfloat16

D_MODEL = 1024
HEAD_DIM = 64
A_HEADS = 8
A_WIDTH = A_HEADS * HEAD_DIM
B_HEADS = 4
B_V_DIM = 2 * HEAD_DIM
B_WIDTH = B_HEADS * B_V_DIM
D_FF = 2816
ROPE_THETA = 10000.0
EPS = 1e-6
NEG_INF = -1e30
DILATIONS = (1, 4, 16)
HALF_WINDOW = 64

LANES = 128
FF_CHUNK = 256
N_FF_CHUNKS = D_FF // FF_CHUNK
ROW_TILE = 512
A_BLOCK_Q = 128
A_BLOCK_K = A_BLOCK_Q + 2 * HALF_WINDOW
B_BLOCK_Q = 256
B_BLOCK_K = 512
VMEM_LIMIT = 56 * 1024 * 1024


def _rms(x, g):
    ms = jnp.mean(x * x, axis=-1, keepdims=True)
    return x * lax.rsqrt(ms + EPS) * g


def _swiglu(hb, wgu_ref, wd_ref, act_ref):
    for c in range(N_FF_CHUNKS):
        gu = jnp.dot(hb, wgu_ref[:, 2 * FF_CHUNK * c:2 * FF_CHUNK * (c + 1)],
                     preferred_element_type=F32)
        g = gu[:, :FF_CHUNK]
        u = gu[:, FF_CHUNK:]
        act_ref[:, FF_CHUNK * c:FF_CHUNK * (c + 1)] = (g * jax.nn.sigmoid(g) * u).astype(BF16)
    return jnp.dot(act_ref[...], wd_ref[...], preferred_element_type=F32)


def _const_spec(shape):
    return pl.BlockSpec(shape, lambda *_: (0,) * len(shape), pipeline_mode=pl.Buffered(1))


def _ffn_kernel(x_ref, g_ref, wgu_ref, wd_ref, o_ref, act_ref):
    x = x_ref[...]
    hb = _rms(x, g_ref[...]).astype(BF16)
    o_ref[...] = x + 0.5 * _swiglu(hb, wgu_ref, wd_ref, act_ref)


def _ffn(x, g, wgu, wd):
    n = x.shape[0]
    return pl.pallas_call(
        _ffn_kernel,
        grid=(n // ROW_TILE,),
        in_specs=[
            pl.BlockSpec((ROW_TILE, D_MODEL), lambda i: (i, 0)),
            _const_spec((1, D_MODEL)),
            _const_spec((D_MODEL, 2 * D_FF)),
            _const_spec((D_FF, D_MODEL)),
        ],
        out_specs=pl.BlockSpec((ROW_TILE, D_MODEL), lambda i: (i, 0)),
        out_shape=jax.ShapeDtypeStruct((n, D_MODEL), F32),
        scratch_shapes=[pltpu.VMEM((ROW_TILE, D_FF), BF16)],
        compiler_params=pltpu.CompilerParams(
            dimension_semantics=("parallel",), vmem_limit_bytes=VMEM_LIMIT),
        name="ffn",
    )(x, g, wgu, wd)


def _in_proj_kernel(x_ref, g_ref, w_ref, hm_ref, gaq_ref, gak_ref, gbq_ref, gbk_ref,
                    cos_ref, sin_ref, aq_ref, ak_ref, av_ref, bq_ref, bk_ref, bv_ref):
    hb = _rms(x_ref[...], g_ref[...]).astype(BF16)
    cos = cos_ref[...]
    sin = sin_ref[...]
    lane = lax.broadcasted_iota(jnp.int32, cos.shape, 1)
    low_half = (lane & (HEAD_DIM - 1)) < (HEAD_DIM // 2)

    def seg(j):
        return jnp.dot(hb, w_ref[:, A_WIDTH * j:A_WIDTH * (j + 1)], preferred_element_type=F32)

    def norm_rope(j, gain_ref, out_ref):
        p = seg(j)
        ms = jnp.dot((p * p).astype(BF16), hm_ref[...], preferred_element_type=F32)
        y = p * lax.rsqrt(ms + EPS) * gain_ref[...]
        for c in range(A_WIDTH // LANES):
            yc = y[:, LANES * c:LANES * (c + 1)]
            rot = jnp.where(low_half, pltpu.roll(yc, LANES - HEAD_DIM // 2, 1),
                            pltpu.roll(yc, HEAD_DIM // 2, 1))
            out_ref[:, LANES * c:LANES * (c + 1)] = (yc * cos + rot * sin).astype(BF16)

    norm_rope(0, gaq_ref, aq_ref)
    norm_rope(1, gak_ref, ak_ref)
    av_ref[...] = seg(2).astype(BF16)
    norm_rope(3, gbq_ref, bq_ref)
    norm_rope(4, gbk_ref, bk_ref)
    bv_ref[...] = seg(5).astype(BF16)


def _in_proj(x1, g, w_in, head_mean, gaq, gak, gbq, gbk, cos, sin, seq):
    n = x1.shape[0]
    tiles_per_seq = seq // ROW_TILE
    row = pl.BlockSpec((ROW_TILE, A_WIDTH), lambda i: (i, 0))
    table = pl.BlockSpec((ROW_TILE, LANES), lambda i: (i % tiles_per_seq, 0))
    gain = _const_spec((1, A_WIDTH))
    out = jax.ShapeDtypeStruct((n, A_WIDTH), BF16)
    return pl.pallas_call(
        _in_proj_kernel,
        grid=(n // ROW_TILE,),
        in_specs=[
            pl.BlockSpec((ROW_TILE, D_MODEL), lambda i: (i, 0)),
            _const_spec((1, D_MODEL)),
            _const_spec((D_MODEL, 6 * A_WIDTH)),
            _const_spec((A_WIDTH, A_WIDTH)),
            gain, gain, gain, gain, table, table,
        ],
        out_specs=[row] * 6,
        out_shape=[out] * 6,
        compiler_params=pltpu.CompilerParams(
            dimension_semantics=("parallel",), vmem_limit_bytes=VMEM_LIMIT),
        name="in_proj",
    )(x1, g, w_in, head_mean, gaq, gak, gbq, gbk, cos, sin)


def _attn_a_kernel(q_ref, k_ref, v_ref, g_ref, o_ref,
                   qf, kf, vf, qd, kd, vd, m_run, l_run, acc_run, *, seq):
    lane = lax.broadcasted_iota(jnp.int32, (A_BLOCK_Q, LANES), 1)
    head0 = lane < HEAD_DIM
    head_masks = (head0.astype(F32), 1.0 - head0.astype(F32))
    qi = lax.broadcasted_iota(jnp.int32, (A_BLOCK_Q, A_BLOCK_K), 0)
    kj = lax.broadcasted_iota(jnp.int32, (A_BLOCK_Q, A_BLOCK_K), 1)
    in_band = jnp.abs(kj - HALF_WINDOW - qi) <= HALF_WINDOW

    copy_rows = 512
    for c in range(seq // copy_rows):
        rows = slice(copy_rows * c, copy_rows * (c + 1))
        qf[rows, :] = q_ref[rows, :].astype(F32)
        kf[rows, :] = k_ref[rows, :].astype(F32)
        vf[rows, :] = v_ref[rows, :].astype(F32)

    offset = 0
    q_offset = 0
    for d in DILATIONS:
        run = seq // d
        padded = run + 2 * HALF_WINDOW
        chunk = min(run, 256)
        zeros = jnp.zeros((HALF_WINDOW, LANES), BF16)
        for r in range(d):
            base = offset + r * padded
            kd[base:base + HALF_WINDOW, :] = zeros
            vd[base:base + HALF_WINDOW, :] = zeros
            kd[base + HALF_WINDOW + run:base + padded, :] = zeros
            vd[base + HALF_WINDOW + run:base + padded, :] = zeros
            for c in range(run // chunk):
                src = pl.ds(r + c * chunk * d, chunk, stride=d) if d > 1 else pl.ds(c * chunk, chunk)
                dst = base + HALF_WINDOW + c * chunk
                qdst = q_offset + r * run + c * chunk
                qd[qdst:qdst + chunk, :] = qf[src, :].astype(BF16)
                kd[dst:dst + chunk, :] = kf[src, :].astype(BF16)
                vd[dst:dst + chunk, :] = vf[src, :].astype(BF16)

        blocks_per_run = run // A_BLOCK_Q

        def block(i, carry, d=d, run=run, offset=offset, q_offset=q_offset,
                  blocks_per_run=blocks_per_run):
            r = i // blocks_per_run
            tb = i % blocks_per_run
            q_start = pl.multiple_of(q_offset + i * A_BLOCK_Q, A_BLOCK_Q)
            k_start = pl.multiple_of(offset + i * A_BLOCK_Q + r * (2 * HALF_WINDOW), A_BLOCK_Q)
            qb = qd[pl.ds(q_start, A_BLOCK_Q), :].astype(F32)
            kw = kd[pl.ds(k_start, A_BLOCK_K), :]
            vw = vd[pl.ds(k_start, A_BLOCK_K), :]
            kt = kj + (tb * A_BLOCK_Q - HALF_WINDOW)
            ok = in_band & (kt >= 0) & (kt < run)
            ms, ls, pvs = [], [], []
            for h in range(2):
                qh = (qb * head_masks[h]).astype(BF16)
                s = lax.dot_general(qh, kw, (((1,), (1,)), ((), ())), preferred_element_type=F32)
                s = jnp.where(ok, s, NEG_INF)
                m = jnp.max(s, axis=-1, keepdims=True)
                p = jnp.exp(s - m)
                ms.append(m)
                ls.append(jnp.sum(p, axis=-1, keepdims=True))
                pvs.append(jnp.dot(p.astype(BF16), vw, preferred_element_type=F32))
            m_new = jnp.where(head0, ms[0], ms[1])
            l_new = jnp.where(head0, ls[0], ls[1])
            acc_new = jnp.where(head0, pvs[0], pvs[1])
            if d == 1:
                idx = pl.ds(pl.multiple_of(i * A_BLOCK_Q, A_BLOCK_Q), A_BLOCK_Q)
                m_run[idx, :] = m_new
                l_run[idx, :] = l_new
                acc_run[idx, :] = acc_new
            else:
                idx = pl.ds(tb * (A_BLOCK_Q * d) + r, A_BLOCK_Q, stride=d)
                m_old = m_run[idx, :]
                m_tot = jnp.maximum(m_old, m_new)
                a_old = jnp.exp(m_old - m_tot)
                a_new = jnp.exp(m_new - m_tot)
                m_run[idx, :] = m_tot
                l_run[idx, :] = a_old * l_run[idx, :] + a_new * l_new
                acc_run[idx, :] = a_old * acc_run[idx, :] + a_new * acc_new
            return carry

        lax.fori_loop(0, seq // A_BLOCK_Q, block, 0)
        offset += d * padded
        q_offset += seq

    gain = g_ref[0]
    out_rows = 256
    lane_o = lax.broadcasted_iota(jnp.int32, (out_rows, LANES), 1)
    head0_o = lane_o < HEAD_DIM

    def finish(c, carry):
        rows = pl.ds(pl.multiple_of(c * out_rows, out_rows), out_rows)
        o = acc_run[rows, :] / l_run[rows, :]
        sq = o * o
        s0 = jnp.sum(jnp.where(head0_o, sq, 0.0), axis=-1, keepdims=True)
        s1 = jnp.sum(jnp.where(head0_o, 0.0, sq), axis=-1, keepdims=True)
        ms = jnp.where(head0_o, s0, s1) * (1.0 / HEAD_DIM)
        o_ref[rows, :] = (o * lax.rsqrt(ms + EPS) * gain).astype(BF16)
        return carry

    lax.fori_loop(0, seq // out_rows, finish, 0)


def _attn_a(aq, ak, av, g_pairs, batch, seq):
    n = aq.shape[0]
    pairs = A_WIDTH // LANES
    blk = pl.BlockSpec((seq, LANES), lambda b, p: (b, p))
    kv_rows = sum(seq + 2 * HALF_WINDOW * d for d in DILATIONS)
    return pl.pallas_call(
        functools.partial(_attn_a_kernel, seq=seq),
        grid=(batch, pairs),
        in_specs=[blk, blk, blk, pl.BlockSpec((1, 1, LANES), lambda b, p: (p, 0, 0))],
        out_specs=blk,
        out_shape=jax.ShapeDtypeStruct((n, A_WIDTH), BF16),
        scratch_shapes=[
            pltpu.VMEM((seq, LANES), F32), pltpu.VMEM((seq, LANES), F32),
            pltpu.VMEM((seq, LANES), F32),
            pltpu.VMEM((len(DILATIONS) * seq, LANES), BF16),
            pltpu.VMEM((kv_rows, LANES), BF16), pltpu.VMEM((kv_rows, LANES), BF16),
            pltpu.VMEM((seq, LANES), F32), pltpu.VMEM((seq, LANES), F32),
            pltpu.VMEM((seq, LANES), F32),
        ],
        compiler_params=pltpu.CompilerParams(
            dimension_semantics=("parallel", "parallel"), vmem_limit_bytes=VMEM_LIMIT),
        name="attn_a",
    )(aq, ak, av, g_pairs)


def _attn_b_kernel(q_ref, k_ref, v_ref, g_ref, lq1_ref, lk1_ref, lq2_ref, lk2_ref, o_ref,
                   vext, s_scr, *, seq, lambda_init):
    @pl.when(pl.program_id(2) == 0)
    def _():
        vext[:, :B_V_DIM] = v_ref[...]
        vext[:, B_V_DIM:] = jnp.ones((seq, B_V_DIM), BF16)

    lane = lax.broadcasted_iota(jnp.int32, (B_BLOCK_Q, LANES), 1)
    comp0 = (lane < HEAD_DIM).astype(F32)
    q = q_ref[...].astype(F32)
    q2 = jnp.concatenate([(q * comp0).astype(BF16), (q * (1.0 - comp0)).astype(BF16)], axis=0)

    n_chunks = seq // B_BLOCK_K
    mx = jnp.full((2 * B_BLOCK_Q, LANES), -jnp.inf, F32)
    for c in range(n_chunks):
        s = lax.dot_general(q2, k_ref[B_BLOCK_K * c:B_BLOCK_K * (c + 1), :],
                            (((1,), (1,)), ((), ())), preferred_element_type=F32)
        s_scr[:, B_BLOCK_K * c:B_BLOCK_K * (c + 1)] = s
        for t in range(B_BLOCK_K // LANES):
            mx = jnp.maximum(mx, s[:, LANES * t:LANES * (t + 1)])
    m = jnp.max(mx, axis=-1, keepdims=True)

    acc = jnp.zeros((2 * B_BLOCK_Q, 2 * B_V_DIM), F32)
    for c in range(n_chunks):
        p = jnp.exp(s_scr[:, B_BLOCK_K * c:B_BLOCK_K * (c + 1)] - m).astype(BF16)
        acc = acc + jnp.dot(p, vext[B_BLOCK_K * c:B_BLOCK_K * (c + 1), :],
                            preferred_element_type=F32)

    o = acc[:, :B_V_DIM] / acc[:, B_V_DIM:B_V_DIM + 1]
    lam = (jnp.exp(jnp.sum(lq1_ref[...] * lk1_ref[...], axis=-1, keepdims=True))
           - jnp.exp(jnp.sum(lq2_ref[...] * lk2_ref[...], axis=-1, keepdims=True)) + lambda_init)
    diff = o[:B_BLOCK_Q] - lam * o[B_BLOCK_Q:]
    o_ref[...] = (_rms(diff, g_ref[...]) * (1.0 - lambda_init)).astype(BF16)


def _attn_b(bq, bk, bv, g_out, lq1, lk1, lq2, lk2, batch, seq, lambda_init):
    n = bq.shape[0]
    q_tiles = seq // B_BLOCK_Q
    qblk = pl.BlockSpec((B_BLOCK_Q, LANES), lambda b, h, i: (b * q_tiles + i, h))
    kvblk = pl.BlockSpec((seq, LANES), lambda b, h, i: (b, h))
    vec = lambda w: pl.BlockSpec((1, w), lambda b, h, i: (0, 0))
    return pl.pallas_call(
        functools.partial(_attn_b_kernel, seq=seq, lambda_init=lambda_init),
        grid=(batch, B_HEADS, q_tiles),
        in_specs=[qblk, kvblk, kvblk, vec(B_V_DIM),
                  vec(HEAD_DIM), vec(HEAD_DIM), vec(HEAD_DIM), vec(HEAD_DIM)],
        out_specs=qblk,
        out_shape=jax.ShapeDtypeStruct((n, B_WIDTH), BF16),
        scratch_shapes=[pltpu.VMEM((seq, 2 * B_V_DIM), BF16),
                        pltpu.VMEM((2 * B_BLOCK_Q, seq), F32)],
        compiler_params=pltpu.CompilerParams(
            dimension_semantics=("parallel", "parallel", "arbitrary"),
            vmem_limit_bytes=VMEM_LIMIT),
        name="attn_b",
    )(bq, bk, bv, g_out, lq1, lk1, lq2, lk2)


def _out_ffn_kernel(x_ref, oa_ref, ob_ref, wo_ref, g2_ref, wgu_ref, wd_ref, gf_ref, o_ref,
                    act_ref):
    x2 = (x_ref[...]
          + jnp.dot(oa_ref[...], wo_ref[:A_WIDTH, :], preferred_element_type=F32)
          + jnp.dot(ob_ref[...], wo_ref[A_WIDTH:, :], preferred_element_type=F32))
    hb = _rms(x2, g2_ref[...]).astype(BF16)
    x3 = x2 + 0.5 * _swiglu(hb, wgu_ref, wd_ref, act_ref)
    o_ref[...] = _rms(x3, gf_ref[...])


def _out_ffn(x1, oa, ob, w_out, g2, wgu, wd, gf):
    n = x1.shape[0]
    row = lambda w: pl.BlockSpec((ROW_TILE, w), lambda i: (i, 0))
    return pl.pallas_call(
        _out_ffn_kernel,
        grid=(n // ROW_TILE,),
        in_specs=[
            row(D_MODEL), row(A_WIDTH), row(B_WIDTH),
            _const_spec((A_WIDTH + B_WIDTH, D_MODEL)),
            _const_spec((1, D_MODEL)),
            _const_spec((D_MODEL, 2 * D_FF)),
            _const_spec((D_FF, D_MODEL)),
            _const_spec((1, D_MODEL)),
        ],
        out_specs=row(D_MODEL),
        out_shape=jax.ShapeDtypeStruct((n, D_MODEL), F32),
        scratch_shapes=[pltpu.VMEM((ROW_TILE, D_FF), BF16)],
        compiler_params=pltpu.CompilerParams(
            dimension_semantics=("parallel",), vmem_limit_bytes=VMEM_LIMIT),
        name="out_ffn",
    )(x1, oa, ob, w_out, g2, wgu, wd, gf)


def _pair_gate_up(w_gu):
    gate = w_gu[:, :D_FF].reshape(D_MODEL, N_FF_CHUNKS, FF_CHUNK)
    up = w_gu[:, D_FF:].reshape(D_MODEL, N_FF_CHUNKS, FF_CHUNK)
    return jnp.concatenate([gate, up], axis=2).reshape(D_MODEL, 2 * D_FF).astype(BF16)


def _rope_tables(seq):
    inv = 1.0 / (ROPE_THETA ** (jnp.arange(0, HEAD_DIM, 2, dtype=F32) / HEAD_DIM))
    ang = jnp.arange(seq, dtype=F32)[:, None] * inv[None, :]
    ang = jnp.concatenate([ang, ang], axis=-1)
    sign = jnp.where(jnp.arange(HEAD_DIM) < HEAD_DIM // 2, -1.0, 1.0).astype(F32)
    cos = jnp.tile(jnp.cos(ang), (1, LANES // HEAD_DIM))
    sin = jnp.tile(jnp.sin(ang) * sign, (1, LANES // HEAD_DIM))
    return cos, sin


def kernel(x_prompt, x_sample, g_ffn1, w_ffn1_gu, w_ffn1_down, g_mix, w_in, g_a_q, g_a_k, g_a_out, g_b_q, g_b_k, lam_q1, lam_k1, lam_q2, lam_k2, g_b_out, w_out, g_ffn2, w_ffn2_gu, w_ffn2_down, g_final):
    depth = g_ffn1.shape[0]
    scale = HEAD_DIM ** -0.5
    head_mean = (jnp.kron(jnp.eye(A_HEADS, dtype=F32), jnp.ones((HEAD_DIM, HEAD_DIM), F32))
                 / HEAD_DIM).astype(BF16)
    row = lambda v: v.reshape(1, -1).astype(F32)
    tile_heads = lambda v: jnp.tile(v.astype(F32), A_WIDTH // HEAD_DIM).reshape(1, A_WIDTH)

    layers = []
    for l in range(depth):
        layers.append(dict(
            g1=row(g_ffn1[l]), wgu1=_pair_gate_up(w_ffn1_gu[l]), wd1=w_ffn1_down[l].astype(BF16),
            gmix=row(g_mix[l]), w_in=w_in[l].astype(BF16),
            gaq=tile_heads(g_a_q[l]) * scale, gak=tile_heads(g_a_k[l]),
            gbq=tile_heads(g_b_q[l]) * scale, gbk=tile_heads(g_b_k[l]),
            ga_out=g_a_out[l].astype(F32).reshape(A_WIDTH // LANES, 1, LANES),
            gb_out=row(g_b_out[l]),
            lq1=row(lam_q1[l]), lk1=row(lam_k1[l]), lq2=row(lam_q2[l]), lk2=row(lam_k2[l]),
            w_out=w_out[l].astype(BF16),
            g2=row(g_ffn2[l]), wgu2=_pair_gate_up(w_ffn2_gu[l]), wd2=w_ffn2_down[l].astype(BF16),
            gf=row(g_final[l]),
            lambda_init=0.8 - 0.6 * math.exp(-0.3 * l),
        ))

    def trunk(x):
        batch, seq, _ = x.shape
        cos, sin = _rope_tables(seq)
        x = x.reshape(batch * seq, D_MODEL)
        for p in layers:
            x1 = _ffn(x, p["g1"], p["wgu1"], p["wd1"])
            aq, ak, av, bq, bk, bv = _in_proj(x1, p["gmix"], p["w_in"], head_mean, p["gaq"],
                                              p["gak"], p["gbq"], p["gbk"], cos, sin, seq)
            oa = _attn_a(aq, ak, av, p["ga_out"], batch, seq)
            ob = _attn_b(bq, bk, bv, p["gb_out"], p["lq1"], p["lk1"], p["lq2"], p["lk2"],
                         batch, seq, p["lambda_init"])
            x = _out_ffn(x1, oa, ob, p["w_out"], p["g2"], p["wgu2"], p["wd2"], p["gf"])
        return x.reshape(batch, seq, D_MODEL)

    return (trunk(x_prompt), trunk(x_sample))
```

```python
import functools
import math

import jax
import jax.numpy as jnp
from jax import lax
from jax.experimental import pallas as pl
from jax.experimental.pallas import tpu as pltpu

F32 = jnp.float32
BF16 = jnp.bfloat16

D_MODEL = 1024
HEAD_DIM = 64
A_HEADS = 8
A_WIDTH = A_HEADS * HEAD_DIM
B_HEADS = 4
B_V_DIM = 2 * HEAD_DIM
B_WIDTH = B_HEADS * B_V_DIM
D_FF = 2816
ROPE_THETA = 10000.0
EPS = 1e-6
NEG_INF = -1e30
DILATIONS = (1, 4, 16)
HALF_WINDOW = 64

LANES = 128
FF_CHUNK = 256
N_FF_CHUNKS = D_FF // FF_CHUNK
ROW_TILE = 512
A_BLOCK_Q = 128
A_BLOCK_K = A_BLOCK_Q + 2 * HALF_WINDOW
A_UNROLL = 8
B_BLOCK_Q = 256
B_BLOCK_K = 512
VMEM_LIMIT = 56 * 1024 * 1024


def _rms(x, g):
    ms = jnp.mean(x * x, axis=-1, keepdims=True)
    return x * lax.rsqrt(ms + EPS) * g


def _swiglu(hb, wgu_ref, wd_ref, act_ref):
    for c in range(N_FF_CHUNKS):
        gu = jnp.dot(hb, wgu_ref[:, 2 * FF_CHUNK * c:2 * FF_CHUNK * (c + 1)],
                     preferred_element_type=F32)
        g = gu[:, :FF_CHUNK]
        u = gu[:, FF_CHUNK:]
        act_ref[:, FF_CHUNK * c:FF_CHUNK * (c + 1)] = (g * jax.nn.sigmoid(g) * u).astype(BF16)
    return jnp.dot(act_ref[...], wd_ref[...], preferred_element_type=F32)


def _const_spec(shape):
    return pl.BlockSpec(shape, lambda *_: (0,) * len(shape), pipeline_mode=pl.Buffered(1))


def _ffn_kernel(x_ref, g_ref, wgu_ref, wd_ref, o_ref, act_ref):
    x = x_ref[...]
    hb = _rms(x, g_ref[...]).astype(BF16)
    o_ref[...] = x + 0.5 * _swiglu(hb, wgu_ref, wd_ref, act_ref)


def _ffn(x, g, wgu, wd):
    n = x.shape[0]
    return pl.pallas_call(
        _ffn_kernel,
        grid=(n // ROW_TILE,),
        in_specs=[
            pl.BlockSpec((ROW_TILE, D_MODEL), lambda i: (i, 0)),
            _const_spec((1, D_MODEL)),
            _const_spec((D_MODEL, 2 * D_FF)),
            _const_spec((D_FF, D_MODEL)),
        ],
        out_specs=pl.BlockSpec((ROW_TILE, D_MODEL), lambda i: (i, 0)),
        out_shape=jax.ShapeDtypeStruct((n, D_MODEL), F32),
        scratch_shapes=[pltpu.VMEM((ROW_TILE, D_FF), BF16)],
        compiler_params=pltpu.CompilerParams(
            dimension_semantics=("parallel",), vmem_limit_bytes=VMEM_LIMIT),
        name="ffn",
    )(x, g, wgu, wd)


def _in_proj_kernel(x_ref, g_ref, w_ref, hm_ref, gaq_ref, gak_ref, gbq_ref, gbk_ref,
                    cos_ref, sin_ref, aq_ref, ak_ref, av_ref, bq_ref, bk_ref, bv_ref):
    hb = _rms(x_ref[...], g_ref[...]).astype(BF16)
    cos = cos_ref[...]
    sin = sin_ref[...]
    lane = lax.broadcasted_iota(jnp.int32, cos.shape, 1)
    low_half = (lane & (HEAD_DIM - 1)) < (HEAD_DIM // 2)

    def seg(j):
        return jnp.dot(hb, w_ref[:, A_WIDTH * j:A_WIDTH * (j + 1)], preferred_element_type=F32)

    def norm_rope(j, gain_ref, out_ref):
        p = seg(j)
        ms = jnp.dot((p * p).astype(BF16), hm_ref[...], preferred_element_type=F32)
        y = p * lax.rsqrt(ms + EPS) * gain_ref[...]
        for c in range(A_WIDTH // LANES):
            yc = y[:, LANES * c:LANES * (c + 1)]
            rot = jnp.where(low_half, pltpu.roll(yc, LANES - HEAD_DIM // 2, 1),
                            pltpu.roll(yc, HEAD_DIM // 2, 1))
            out_ref[:, LANES * c:LANES * (c + 1)] = (yc * cos + rot * sin).astype(BF16)

    norm_rope(0, gaq_ref, aq_ref)
    norm_rope(1, gak_ref, ak_ref)
    av_ref[...] = seg(2).astype(BF16)
    norm_rope(3, gbq_ref, bq_ref)
    norm_rope(4, gbk_ref, bk_ref)
    bv_ref[...] = seg(5).astype(BF16)


def _in_proj(x1, g, w_in, head_mean, gaq, gak, gbq, gbk, cos, sin, seq):
    n = x1.shape[0]
    tiles_per_seq = seq // ROW_TILE
    row = pl.BlockSpec((ROW_TILE, A_WIDTH), lambda i: (i, 0))
    table = pl.BlockSpec((ROW_TILE, LANES), lambda i: (i % tiles_per_seq, 0))
    gain = _const_spec((1, A_WIDTH))
    out = jax.ShapeDtypeStruct((n, A_WIDTH), BF16)
    return pl.pallas_call(
        _in_proj_kernel,
        grid=(n // ROW_TILE,),
        in_specs=[
            pl.BlockSpec((ROW_TILE, D_MODEL), lambda i: (i, 0)),
            _const_spec((1, D_MODEL)),
            _const_spec((D_MODEL, 6 * A_WIDTH)),
            _const_spec((A_WIDTH, A_WIDTH)),
            gain, gain, gain, gain, table, table,
        ],
        out_specs=[row] * 6,
        out_shape=[out] * 6,
        compiler_params=pltpu.CompilerParams(
            dimension_semantics=("parallel",), vmem_limit_bytes=VMEM_LIMIT),
        name="in_proj",
    )(x1, g, w_in, head_mean, gaq, gak, gbq, gbk, cos, sin)


def _attn_a_kernel(q_ref, k_ref, v_ref, g_ref, o_ref,
                   qf, kf, vf, qd, kd, vd, bias, m_run, l_run, acc_run, *, seq):
    lane = lax.broadcasted_iota(jnp.int32, (A_BLOCK_Q, LANES), 1)
    head0 = lane < HEAD_DIM
    qi = lax.broadcasted_iota(jnp.int32, (A_BLOCK_Q, A_BLOCK_K), 0)
    kj = lax.broadcasted_iota(jnp.int32, (A_BLOCK_Q, A_BLOCK_K), 1)
    in_band = jnp.abs(kj - HALF_WINDOW - qi) <= HALF_WINDOW
    ones_cols = jnp.ones((A_BLOCK_K, LANES), BF16)
    for variant in range(4):
        ok = in_band
        if variant & 1:
            ok = ok & (kj >= HALF_WINDOW)
        if variant & 2:
            ok = ok & (kj < A_BLOCK_Q + HALF_WINDOW)
        b = jnp.where(ok, 0.0, NEG_INF)
        bias[variant, :A_BLOCK_Q, :] = b
        bias[variant, A_BLOCK_Q:, :] = b

    copy_rows = 512
    for c in range(seq // copy_rows):
        rows = slice(copy_rows * c, copy_rows * (c + 1))
        qf[rows, :] = q_ref[rows, :].astype(F32)
        kf[rows, :] = k_ref[rows, :].astype(F32)
        vf[rows, :] = v_ref[rows, :].astype(F32)

    offset = 0
    q_offset = 0
    for d in DILATIONS:
        run = seq // d
        padded = run + 2 * HALF_WINDOW
        chunk = min(run, 256)
        zeros = jnp.zeros((HALF_WINDOW, LANES), BF16)
        head0_c = lax.broadcasted_iota(jnp.int32, (chunk, LANES), 1) < HEAD_DIM
        for r in range(d):
            base = offset + r * padded
            kd[base:base + HALF_WINDOW, :] = zeros
            vd[base:base + HALF_WINDOW, :] = zeros
            kd[base + HALF_WINDOW + run:base + padded, :] = zeros
            vd[base + HALF_WINDOW + run:base + padded, :] = zeros
            for c in range(run // chunk):
                dst = base + HALF_WINDOW + c * chunk
                qdst = q_offset + r * run + c * chunk
                if d == 1:
                    src = pl.ds(c * chunk, chunk)
                    qx = q_ref[src, :].astype(F32)
                    kx = k_ref[src, :]
                    vx = v_ref[src, :]
                else:
                    src = pl.ds(r + c * chunk * d, chunk, stride=d)
                    qx = qf[src, :]
                    kx = kf[src, :].astype(BF16)
                    vx = vf[src, :].astype(BF16)
                qd[0, qdst:qdst + chunk, :] = jnp.where(head0_c, qx, 0.0).astype(BF16)
                qd[1, qdst:qdst + chunk, :] = jnp.where(head0_c, 0.0, qx).astype(BF16)
                kd[dst:dst + chunk, :] = kx
                vd[dst:dst + chunk, :] = vx

        blocks_per_run = run // A_BLOCK_Q

        def block(i, carry, d=d, offset=offset, q_offset=q_offset,
                  blocks_per_run=blocks_per_run):
            r = i // blocks_per_run
            tb = i % blocks_per_run
            q_rows = pl.ds(pl.multiple_of(q_offset + i * A_BLOCK_Q, A_BLOCK_Q), A_BLOCK_Q)
            k_start = pl.multiple_of(offset + i * A_BLOCK_Q + r * (2 * HALF_WINDOW), A_BLOCK_Q)
            q2 = jnp.concatenate([qd[0, q_rows, :], qd[1, q_rows, :]], axis=0)
            kw = kd[pl.ds(k_start, A_BLOCK_K), :]
            vw = vd[pl.ds(k_start, A_BLOCK_K), :]
            variant = jnp.where(tb == 0, 1, 0) + jnp.where(tb == blocks_per_run - 1, 2, 0)
            s = lax.dot_general(q2, kw, (((1,), (1,)), ((), ())), preferred_element_type=F32)
            s = s + bias[variant]
            m = jnp.max(s, axis=-1, keepdims=True)
            p = jnp.exp(s - m).astype(BF16)
            pv = jnp.dot(p, jnp.concatenate([vw, ones_cols], axis=1), preferred_element_type=F32)
            m_new = jnp.where(head0, m[:A_BLOCK_Q], m[A_BLOCK_Q:])
            l_new = jnp.where(head0, pv[:A_BLOCK_Q, LANES:], pv[A_BLOCK_Q:, LANES:])
            acc_new = jnp.where(head0, pv[:A_BLOCK_Q, :LANES], pv[A_BLOCK_Q:, :LANES])
            if d == 1:
                idx = pl.ds(pl.multiple_of(i * A_BLOCK_Q, A_BLOCK_Q), A_BLOCK_Q)
                m_run[idx, :] = m_new
                l_run[idx, :] = l_new
                acc_run[idx, :] = acc_new
            else:
                idx = pl.ds(tb * (A_BLOCK_Q * d) + r, A_BLOCK_Q, stride=d)
                m_old = m_run[idx, :]
                m_tot = jnp.maximum(m_old, m_new)
                a_old = jnp.exp(m_old - m_tot)
                a_new = jnp.exp(m_new - m_tot)
                m_run[idx, :] = m_tot
                l_run[idx, :] = a_old * l_run[idx, :] + a_new * l_new
                acc_run[idx, :] = a_old * acc_run[idx, :] + a_new * acc_new
            return carry

        lax.fori_loop(0, seq // A_BLOCK_Q, block, 0, unroll=A_UNROLL)
        offset += d * padded
        q_offset += seq

    gain = g_ref[0]
    out_rows = 256
    lane_o = lax.broadcasted_iota(jnp.int32, (out_rows, LANES), 1)
    head0_o = lane_o < HEAD_DIM

    def finish(c, carry):
        rows = pl.ds(pl.multiple_of(c * out_rows, out_rows), out_rows)
        o = acc_run[rows, :] / l_run[rows, :]
        sq = o * o
        s0 = jnp.sum(jnp.where(head0_o, sq, 0.0), axis=-1, keepdims=True)
        s1 = jnp.sum(jnp.where(head0_o, 0.0, sq), axis=-1, keepdims=True)
        ms = jnp.where(head0_o, s0, s1) * (1.0 / HEAD_DIM)
        o_ref[rows, :] = (o * lax.rsqrt(ms + EPS) * gain).astype(BF16)
        return carry

    lax.fori_loop(0, seq // out_rows, finish, 0)


def _attn_a(aq, ak, av, g_pairs, batch, seq):
    n = aq.shape[0]
    pairs = A_WIDTH // LANES
    blk = pl.BlockSpec((seq, LANES), lambda b, p: (b, p))
    kv_rows = sum(seq + 2 * HALF_WINDOW * d for d in DILATIONS)
    return pl.pallas_call(
        functools.partial(_attn_a_kernel, seq=seq),
        grid=(batch, pairs),
        in_specs=[blk, blk, blk, pl.BlockSpec((1, 1, LANES), lambda b, p: (p, 0, 0))],
        out_specs=blk,
        out_shape=jax.ShapeDtypeStruct((n, A_WIDTH), BF16),
        scratch_shapes=[
            pltpu.VMEM((seq, LANES), F32), pltpu.VMEM((seq, LANES), F32),
            pltpu.VMEM((seq, LANES), F32),
            pltpu.VMEM((2, len(DILATIONS) * seq, LANES), BF16),
            pltpu.VMEM((kv_rows, LANES), BF16), pltpu.VMEM((kv_rows, LANES), BF16),
            pltpu.VMEM((4, 2 * A_BLOCK_Q, A_BLOCK_K), F32),
            pltpu.VMEM((seq, LANES), F32), pltpu.VMEM((seq, LANES), F32),
            pltpu.VMEM((seq, LANES), F32),
        ],
        compiler_params=pltpu.CompilerParams(
            dimension_semantics=("parallel", "parallel"), vmem_limit_bytes=VMEM_LIMIT),
        name="attn_a",
    )(aq, ak, av, g_pairs)


def _attn_b_kernel(q_ref, k_ref, v_ref, g_ref, lq1_ref, lk1_ref, lq2_ref, lk2_ref, o_ref,
                   vext, s_even, s_odd, m_even, m_odd, *, seq, lambda_init):
    vext[:, :B_V_DIM] = v_ref[...]
    vext[:, B_V_DIM:] = jnp.ones((seq, B_V_DIM), BF16)

    lane = lax.broadcasted_iota(jnp.int32, (B_BLOCK_Q, LANES), 1)
    comp0 = (lane < HEAD_DIM).astype(F32)
    comp1 = 1.0 - comp0
    lam = (jnp.exp(jnp.sum(lq1_ref[...] * lk1_ref[...], axis=-1, keepdims=True))
           - jnp.exp(jnp.sum(lq2_ref[...] * lk2_ref[...], axis=-1, keepdims=True)) + lambda_init)
    gain = g_ref[...]
    n_chunks = seq // B_BLOCK_K
    n_tiles = seq // B_BLOCK_Q

    def tile_rows(t):
        return pl.ds(pl.multiple_of(t * B_BLOCK_Q, B_BLOCK_Q), B_BLOCK_Q)

    def stacked_q(t):
        q = q_ref[tile_rows(t), :].astype(F32)
        return jnp.concatenate([(q * comp0).astype(BF16), (q * comp1).astype(BF16)], axis=0)

    def score_chunk(q2, s_ref, c, mx):
        cols = slice(B_BLOCK_K * c, B_BLOCK_K * (c + 1))
        s = lax.dot_general(q2, k_ref[cols, :], (((1,), (1,)), ((), ())),
                            preferred_element_type=F32)
        s_ref[:, cols] = s
        for j in range(B_BLOCK_K // LANES):
            mx = jnp.maximum(mx, s[:, LANES * j:LANES * (j + 1)])
        return mx

    def value_chunk(s_ref, c, m, acc):
        cols = slice(B_BLOCK_K * c, B_BLOCK_K * (c + 1))
        p = jnp.exp(s_ref[:, cols] - m).astype(BF16)
        return acc + jnp.dot(p, vext[cols, :], preferred_element_type=F32)

    def finish_tile(t, acc):
        o = acc[:, :B_V_DIM] / acc[:, B_V_DIM:B_V_DIM + 1]
        diff = o[:B_BLOCK_Q] - lam * o[B_BLOCK_Q:]
        o_ref[tile_rows(t), :] = (_rms(diff, gain) * (1.0 - lambda_init)).astype(BF16)

    def run(t_scores, t_values, scores_to_odd):
        s_w, m_w = (s_odd, m_odd) if scores_to_odd else (s_even, m_even)
        s_r, m_r = (s_even, m_even) if scores_to_odd else (s_odd, m_odd)
        if t_scores is not None:
            q2 = stacked_q(t_scores)
            mx = jnp.full((2 * B_BLOCK_Q, LANES), -jnp.inf, F32)
        if t_values is not None:
            m = m_r[...]
            acc = jnp.zeros((2 * B_BLOCK_Q, 2 * B_V_DIM), F32)
        for c in range(n_chunks):
            if t_values is not None:
                acc = value_chunk(s_r, c, m, acc)
            if t_scores is not None:
                mx = score_chunk(q2, s_w, c, mx)
        if t_scores is not None:
            m_w[...] = jnp.max(mx, axis=-1, keepdims=True)
        if t_values is not None:
            finish_tile(t_values, acc)

    run(0, None, False)

    def body(i, carry):
        run(2 * i + 1, 2 * i, True)
        run(2 * i + 2, 2 * i + 1, False)
        return carry

    lax.fori_loop(0, n_tiles // 2 - 1, body, 0)
    run(n_tiles - 1, n_tiles - 2, True)
    run(None, n_tiles - 1, False)


def _attn_b(bq, bk, bv, g_out, lq1, lk1, lq2, lk2, batch, seq, lambda_init):
    n = bq.shape[0]
    blk = pl.BlockSpec((seq, LANES), lambda b, h: (b, h))
    vec = lambda w: pl.BlockSpec((1, w), lambda b, h: (0, 0))
    return pl.pallas_call(
        functools.partial(_attn_b_kernel, seq=seq, lambda_init=lambda_init),
        grid=(batch, B_HEADS),
        in_specs=[blk, blk, blk, vec(B_V_DIM),
                  vec(HEAD_DIM), vec(HEAD_DIM), vec(HEAD_DIM), vec(HEAD_DIM)],
        out_specs=blk,
        out_shape=jax.ShapeDtypeStruct((n, B_WIDTH), BF16),
        scratch_shapes=[pltpu.VMEM((seq, 2 * B_V_DIM), BF16),
                        pltpu.VMEM((2 * B_BLOCK_Q, seq), F32),
                        pltpu.VMEM((2 * B_BLOCK_Q, seq), F32),
                        pltpu.VMEM((2 * B_BLOCK_Q, 1), F32),
                        pltpu.VMEM((2 * B_BLOCK_Q, 1), F32)],
        compiler_params=pltpu.CompilerParams(
            dimension_semantics=("parallel", "parallel"), vmem_limit_bytes=VMEM_LIMIT),
        name="attn_b",
    )(bq, bk, bv, g_out, lq1, lk1, lq2, lk2)


def _out_ffn_kernel(x_ref, oa_ref, ob_ref, wo_ref, g2_ref, wgu_ref, wd_ref, gf_ref, o_ref,
                    act_ref):
    x2 = (x_ref[...]
          + jnp.dot(oa_ref[...], wo_ref[:A_WIDTH, :], preferred_element_type=F32)
          + jnp.dot(ob_ref[...], wo_ref[A_WIDTH:, :], preferred_element_type=F32))
    hb = _rms(x2, g2_ref[...]).astype(BF16)
    x3 = x2 + 0.5 * _swiglu(hb, wgu_ref, wd_ref, act_ref)
    o_ref[...] = _rms(x3, gf_ref[...])


def _out_ffn(x1, oa, ob, w_out, g2, wgu, wd, gf):
    n = x1.shape[0]
    row = lambda w: pl.BlockSpec((ROW_TILE, w), lambda i: (i, 0))
    return pl.pallas_call(
        _out_ffn_kernel,
        grid=(n // ROW_TILE,),
        in_specs=[
            row(D_MODEL), row(A_WIDTH), row(B_WIDTH),
            _const_spec((A_WIDTH + B_WIDTH, D_MODEL)),
            _const_spec((1, D_MODEL)),
            _const_spec((D_MODEL, 2 * D_FF)),
            _const_spec((D_FF, D_MODEL)),
            _const_spec((1, D_MODEL)),
        ],
        out_specs=row(D_MODEL),
        out_shape=jax.ShapeDtypeStruct((n, D_MODEL), F32),
        scratch_shapes=[pltpu.VMEM((ROW_TILE, D_FF), BF16)],
        compiler_params=pltpu.CompilerParams(
            dimension_semantics=("parallel",), vmem_limit_bytes=VMEM_LIMIT),
        name="out_ffn",
    )(x1, oa, ob, w_out, g2, wgu, wd, gf)


def _pair_gate_up(w_gu):
    gate = w_gu[:, :D_FF].reshape(D_MODEL, N_FF_CHUNKS, FF_CHUNK)
    up = w_gu[:, D_FF:].reshape(D_MODEL, N_FF_CHUNKS, FF_CHUNK)
    return jnp.concatenate([gate, up], axis=2).reshape(D_MODEL, 2 * D_FF).astype(BF16)


def _rope_tables(seq):
    inv = 1.0 / (ROPE_THETA ** (jnp.arange(0, HEAD_DIM, 2, dtype=F32) / HEAD_DIM))
    ang = jnp.arange(seq, dtype=F32)[:, None] * inv[None, :]
    ang = jnp.concatenate([ang, ang], axis=-1)
    sign = jnp.where(jnp.arange(HEAD_DIM) < HEAD_DIM // 2, -1.0, 1.0).astype(F32)
    cos = jnp.tile(jnp.cos(ang), (1, LANES // HEAD_DIM))
    sin = jnp.tile(jnp.sin(ang) * sign, (1, LANES // HEAD_DIM))
    return cos, sin


def kernel(x_prompt, x_sample, g_ffn1, w_ffn1_gu, w_ffn1_down, g_mix, w_in, g_a_q, g_a_k, g_a_out, g_b_q, g_b_k, lam_q1, lam_k1, lam_q2, lam_k2, g_b_out, w_out, g_ffn2, w_ffn2_gu, w_ffn2_down, g_final):
    depth = g_ffn1.shape[0]
    scale = HEAD_DIM ** -0.5
    head_mean = (jnp.kron(jnp.eye(A_HEADS, dtype=F32), jnp.ones((HEAD_DIM, HEAD_DIM), F32))
                 / HEAD_DIM).astype(BF16)
    row = lambda v: v.reshape(1, -1).astype(F32)
    tile_heads = lambda v: jnp.tile(v.astype(F32), A_WIDTH // HEAD_DIM).reshape(1, A_WIDTH)

    layers = []
    for l in range(depth):
        layers.append(dict(
            g1=row(g_ffn1[l]), wgu1=_pair_gate_up(w_ffn1_gu[l]), wd1=w_ffn1_down[l].astype(BF16),
            gmix=row(g_mix[l]), w_in=w_in[l].astype(BF16),
            gaq=tile_heads(g_a_q[l]) * scale, gak=tile_heads(g_a_k[l]),
            gbq=tile_heads(g_b_q[l]) * scale, gbk=tile_heads(g_b_k[l]),
            ga_out=g_a_out[l].astype(F32).reshape(A_WIDTH // LANES, 1, LANES),
            gb_out=row(g_b_out[l]),
            lq1=row(lam_q1[l]), lk1=row(lam_k1[l]), lq2=row(lam_q2[l]), lk2=row(lam_k2[l]),
            w_out=w_out[l].astype(BF16),
            g2=row(g_ffn2[l]), wgu2=_pair_gate_up(w_ffn2_gu[l]), wd2=w_ffn2_down[l].astype(BF16),
            gf=row(g_final[l]),
            lambda_init=0.8 - 0.6 * math.exp(-0.3 * l),
        ))

    def trunk(x):
        batch, seq, _ = x.shape
        cos, sin = _rope_tables(seq)
        x = x.reshape(batch * seq, D_MODEL)
        for p in layers:
            x1 = _ffn(x, p["g1"], p["wgu1"], p["wd1"])
            aq, ak, av, bq, bk, bv = _in_proj(x1, p["gmix"], p["w_in"], head_mean, p["gaq"],
                                              p["gak"], p["gbq"], p["gbk"], cos, sin, seq)
            oa = _attn_a(aq, ak, av, p["ga_out"], batch, seq)
            ob = _attn_b(bq, bk, bv, p["gb_out"], p["lq1"], p["lk1"], p["lq2"], p["lk2"],
                         batch, seq, p["lambda_init"])
            x = _out_ffn(x1, oa, ob, p["w_out"], p["g2"], p["wgu2"], p["wd2"], p["gf"])
        return x.reshape(batch, seq, D_MODEL)

    return (trunk(x_prompt), trunk(x_sample))
```

```python
import functools
import math

import jax
import jax.numpy as jnp
from jax import lax
from jax.experimental import pallas as pl
from jax.experimental.pallas import tpu as pltpu

F32 = jnp.float32
BF16 = jnp.bfloat16

D_MODEL = 1024
HEAD_DIM = 64
A_HEADS = 8
A_WIDTH = A_HEADS * HEAD_DIM
B_HEADS = 4
B_V_DIM = 2 * HEAD_DIM
B_WIDTH = B_HEADS * B_V_DIM
D_FF = 2816
ROPE_THETA = 10000.0
EPS = 1e-6
NEG_INF = -1e30
DILATIONS = (1, 4, 16)
HALF_WINDOW = 64

LANES = 128
FF_CHUNK = 256
N_FF_CHUNKS = D_FF // FF_CHUNK
ROW_TILE = 512
A_BLOCK_Q = 128
A_BLOCK_K = A_BLOCK_Q + 2 * HALF_WINDOW
A_UNROLL = 8
B_BLOCK_Q = 256
B_BLOCK_K = 512
VMEM_LIMIT = 56 * 1024 * 1024


def _rms(x, g):
    ms = jnp.mean(x * x, axis=-1, keepdims=True)
    return x * lax.rsqrt(ms + EPS) * g


def _swiglu(hb, wgu_ref, wd_ref, act_ref):
    for c in range(N_FF_CHUNKS):
        cols = slice(FF_CHUNK * c, FF_CHUNK * (c + 1))
        g = jnp.dot(hb, wgu_ref[:, cols], preferred_element_type=F32)
        u = jnp.dot(hb, wgu_ref[:, D_FF + FF_CHUNK * c:D_FF + FF_CHUNK * (c + 1)],
                    preferred_element_type=F32)
        act_ref[:, cols] = (g * jax.nn.sigmoid(g) * u).astype(BF16)
    return jnp.dot(act_ref[...], wd_ref[...], preferred_element_type=F32)


def _const_spec(shape):
    return pl.BlockSpec(shape, lambda *_: (0,) * len(shape), pipeline_mode=pl.Buffered(1))


def _ffn_kernel(x_ref, g_ref, wgu_ref, wd_ref, o_ref, act_ref):
    x = x_ref[...]
    hb = _rms(x, g_ref[...]).astype(BF16)
    o_ref[...] = x + 0.5 * _swiglu(hb, wgu_ref, wd_ref, act_ref)


def _ffn(x, g, wgu, wd):
    n = x.shape[0]
    return pl.pallas_call(
        _ffn_kernel,
        grid=(n // ROW_TILE,),
        in_specs=[
            pl.BlockSpec((ROW_TILE, D_MODEL), lambda i: (i, 0)),
            _const_spec((1, D_MODEL)),
            _const_spec((D_MODEL, 2 * D_FF)),
            _const_spec((D_FF, D_MODEL)),
        ],
        out_specs=pl.BlockSpec((ROW_TILE, D_MODEL), lambda i: (i, 0)),
        out_shape=jax.ShapeDtypeStruct((n, D_MODEL), F32),
        scratch_shapes=[pltpu.VMEM((ROW_TILE, D_FF), BF16)],
        compiler_params=pltpu.CompilerParams(
            dimension_semantics=("parallel",), vmem_limit_bytes=VMEM_LIMIT),
        name="ffn",
    )(x, g, wgu, wd)


def _in_proj_kernel(x_ref, g_ref, w_ref, hm_ref, gaq_ref, gak_ref, gbq_ref, gbk_ref,
                    cos_ref, sin_ref, aq_ref, ak_ref, av_ref, bq_ref, bk_ref, bv_ref):
    hb = _rms(x_ref[...], g_ref[...]).astype(BF16)
    cos = cos_ref[...]
    sin = sin_ref[...]
    lane = lax.broadcasted_iota(jnp.int32, cos.shape, 1)
    low_half = (lane & (HEAD_DIM - 1)) < (HEAD_DIM // 2)

    def seg(j):
        return jnp.dot(hb, w_ref[:, A_WIDTH * j:A_WIDTH * (j + 1)], preferred_element_type=F32)

    def norm_rope(j, gain_ref, out_ref):
        p = seg(j)
        ms = jnp.dot((p * p).astype(BF16), hm_ref[...], preferred_element_type=F32)
        y = p * lax.rsqrt(ms + EPS) * gain_ref[...]
        for c in range(A_WIDTH // LANES):
            yc = y[:, LANES * c:LANES * (c + 1)]
            rot = jnp.where(low_half, pltpu.roll(yc, LANES - HEAD_DIM // 2, 1),
                            pltpu.roll(yc, HEAD_DIM // 2, 1))
            out_ref[:, LANES * c:LANES * (c + 1)] = (yc * cos + rot * sin).astype(BF16)

    norm_rope(0, gaq_ref, aq_ref)
    norm_rope(1, gak_ref, ak_ref)
    av_ref[...] = seg(2).astype(BF16)
    norm_rope(3, gbq_ref, bq_ref)
    norm_rope(4, gbk_ref, bk_ref)
    bv_ref[...] = seg(5).astype(BF16)


def _in_proj(x1, g, w_in, head_mean, gaq, gak, gbq, gbk, cos, sin, seq):
    n = x1.shape[0]
    tiles_per_seq = seq // ROW_TILE
    row = pl.BlockSpec((ROW_TILE, A_WIDTH), lambda i: (i, 0))
    table = pl.BlockSpec((ROW_TILE, LANES), lambda i: (i % tiles_per_seq, 0))
    gain = _const_spec((1, A_WIDTH))
    out = jax.ShapeDtypeStruct((n, A_WIDTH), BF16)
    return pl.pallas_call(
        _in_proj_kernel,
        grid=(n // ROW_TILE,),
        in_specs=[
            pl.BlockSpec((ROW_TILE, D_MODEL), lambda i: (i, 0)),
            _const_spec((1, D_MODEL)),
            _const_spec((D_MODEL, 6 * A_WIDTH)),
            _const_spec((A_WIDTH, A_WIDTH)),
            gain, gain, gain, gain, table, table,
        ],
        out_specs=[row] * 6,
        out_shape=[out] * 6,
        compiler_params=pltpu.CompilerParams(
            dimension_semantics=("parallel",), vmem_limit_bytes=VMEM_LIMIT),
        name="in_proj",
    )(x1, g, w_in, head_mean, gaq, gak, gbq, gbk, cos, sin)


def _attn_a_kernel(q_ref, k_ref, v_ref, g_ref, o_ref,
                   qf, kf, vf, qg, kg, vg, qd, kd, vd, bias, m_run, l_run, acc_run, *, seq):
    lane = lax.broadcasted_iota(jnp.int32, (A_BLOCK_Q, LANES), 1)
    head0 = lane < HEAD_DIM
    qi = lax.broadcasted_iota(jnp.int32, (A_BLOCK_Q, A_BLOCK_K), 0)
    kj = lax.broadcasted_iota(jnp.int32, (A_BLOCK_Q, A_BLOCK_K), 1)
    in_band = jnp.abs(kj - HALF_WINDOW - qi) <= HALF_WINDOW
    ones_cols = jnp.ones((A_BLOCK_K, LANES), BF16)
    for variant in range(4):
        ok = in_band
        if variant & 1:
            ok = ok & (kj >= HALF_WINDOW)
        if variant & 2:
            ok = ok & (kj < A_BLOCK_Q + HALF_WINDOW)
        b = jnp.where(ok, 0.0, NEG_INF)
        bias[variant, :A_BLOCK_Q, :] = b
        bias[variant, A_BLOCK_Q:, :] = b

    copy_rows = 512
    for c in range(seq // copy_rows):
        rows = slice(copy_rows * c, copy_rows * (c + 1))
        qf[rows, :] = q_ref[rows, :].astype(F32)
        kf[rows, :] = k_ref[rows, :].astype(F32)
        vf[rows, :] = v_ref[rows, :].astype(F32)

    offset = 0
    q_offset = 0
    prev = 1
    source = (qf, kf, vf)
    for d in DILATIONS:
        run = seq // d
        padded = run + 2 * HALF_WINDOW
        chunk = min(run, 256)
        step = d // prev
        keep = (qg, kg, vg) if 1 < d < DILATIONS[-1] else None
        zeros = jnp.zeros((HALF_WINDOW, LANES), BF16)
        head0_c = lax.broadcasted_iota(jnp.int32, (chunk, LANES), 1) < HEAD_DIM
        for r in range(d):
            base = offset + r * padded
            kd[base:base + HALF_WINDOW, :] = zeros
            vd[base:base + HALF_WINDOW, :] = zeros
            kd[base + HALF_WINDOW + run:base + padded, :] = zeros
            vd[base + HALF_WINDOW + run:base + padded, :] = zeros
            for c in range(run // chunk):
                dst = base + HALF_WINDOW + c * chunk
                qdst = q_offset + r * run + c * chunk
                if d == 1:
                    src = pl.ds(c * chunk, chunk)
                    qx = q_ref[src, :].astype(F32)
                    kx = k_ref[src, :]
                    vx = v_ref[src, :]
                else:
                    src = pl.ds((r % prev) * (seq // prev) + r // prev + c * chunk * step,
                                chunk, stride=step)
                    qx, kx, vx = (ref[src, :] for ref in source)
                    if keep is not None:
                        rows = slice(r * run + c * chunk, r * run + (c + 1) * chunk)
                        for ref, x in zip(keep, (qx, kx, vx)):
                            ref[rows, :] = x
                    kx = kx.astype(BF16)
                    vx = vx.astype(BF16)
                qd[0, qdst:qdst + chunk, :] = jnp.where(head0_c, qx, 0.0).astype(BF16)
                qd[1, qdst:qdst + chunk, :] = jnp.where(head0_c, 0.0, qx).astype(BF16)
                kd[dst:dst + chunk, :] = kx
                vd[dst:dst + chunk, :] = vx
        if keep is not None:
            source = keep
        prev = d

        blocks_per_run = run // A_BLOCK_Q

        def block(i, carry, d=d, offset=offset, q_offset=q_offset,
                  blocks_per_run=blocks_per_run):
            r = i // blocks_per_run
            tb = i % blocks_per_run
            q_rows = pl.ds(pl.multiple_of(q_offset + i * A_BLOCK_Q, A_BLOCK_Q), A_BLOCK_Q)
            k_start = pl.multiple_of(offset + i * A_BLOCK_Q + r * (2 * HALF_WINDOW), A_BLOCK_Q)
            q2 = jnp.concatenate([qd[0, q_rows, :], qd[1, q_rows, :]], axis=0)
            kw = kd[pl.ds(k_start, A_BLOCK_K), :]
            vw = vd[pl.ds(k_start, A_BLOCK_K), :]
            variant = jnp.where(tb == 0, 1, 0) + jnp.where(tb == blocks_per_run - 1, 2, 0)
            s = lax.dot_general(q2, kw, (((1,), (1,)), ((), ())), preferred_element_type=F32)
            s = s + bias[variant]
            m = jnp.max(s, axis=-1, keepdims=True)
            p = jnp.exp(s - m).astype(BF16)
            pv = jnp.dot(p, jnp.concatenate([vw, ones_cols], axis=1), preferred_element_type=F32)
            m_new = jnp.where(head0, m[:A_BLOCK_Q], m[A_BLOCK_Q:])
            l_new = jnp.where(head0, pv[:A_BLOCK_Q, LANES:], pv[A_BLOCK_Q:, LANES:])
            acc_new = jnp.where(head0, pv[:A_BLOCK_Q, :LANES], pv[A_BLOCK_Q:, :LANES])
            if d == 1:
                idx = pl.ds(pl.multiple_of(i * A_BLOCK_Q, A_BLOCK_Q), A_BLOCK_Q)
                m_run[idx, :] = m_new
                l_run[idx, :] = l_new
                acc_run[idx, :] = acc_new
            else:
                idx = pl.ds(tb * (A_BLOCK_Q * d) + r, A_BLOCK_Q, stride=d)
                m_old = m_run[idx, :]
                m_tot = jnp.maximum(m_old, m_new)
                a_old = jnp.exp(m_old - m_tot)
                a_new = jnp.exp(m_new - m_tot)
                m_run[idx, :] = m_tot
                l_run[idx, :] = a_old * l_run[idx, :] + a_new * l_new
                acc_run[idx, :] = a_old * acc_run[idx, :] + a_new * acc_new
            return carry

        lax.fori_loop(0, seq // A_BLOCK_Q, block, 0, unroll=A_UNROLL)
        offset += d * padded
        q_offset += seq

    gain = g_ref[0]
    out_rows = 256
    same_head = ((lax.broadcasted_iota(jnp.int32, (LANES, LANES), 0) < HEAD_DIM)
                 == (lax.broadcasted_iota(jnp.int32, (LANES, LANES), 1) < HEAD_DIM))
    head_mean = jnp.where(same_head, 1.0 / HEAD_DIM, 0.0).astype(BF16)

    def finish(c, carry):
        rows = pl.ds(pl.multiple_of(c * out_rows, out_rows), out_rows)
        o = acc_run[rows, :] / l_run[rows, :]
        ms = jnp.dot((o * o).astype(BF16), head_mean, preferred_element_type=F32)
        o_ref[rows, :] = (o * lax.rsqrt(ms + EPS) * gain).astype(BF16)
        return carry

    lax.fori_loop(0, seq // out_rows, finish, 0, unroll=2)


def _attn_a(aq, ak, av, g_pairs, batch, seq):
    n = aq.shape[0]
    pairs = A_WIDTH // LANES
    blk = pl.BlockSpec((seq, LANES), lambda b, p: (b, p))
    kv_rows = sum(seq + 2 * HALF_WINDOW * d for d in DILATIONS)
    return pl.pallas_call(
        functools.partial(_attn_a_kernel, seq=seq),
        grid=(batch, pairs),
        in_specs=[blk, blk, blk, pl.BlockSpec((1, 1, LANES), lambda b, p: (p, 0, 0))],
        out_specs=blk,
        out_shape=jax.ShapeDtypeStruct((n, A_WIDTH), BF16),
        scratch_shapes=[pltpu.VMEM((seq, LANES), F32)] * 6 + [
            pltpu.VMEM((2, len(DILATIONS) * seq, LANES), BF16),
            pltpu.VMEM((kv_rows, LANES), BF16), pltpu.VMEM((kv_rows, LANES), BF16),
            pltpu.VMEM((4, 2 * A_BLOCK_Q, A_BLOCK_K), F32),
            pltpu.VMEM((seq, LANES), F32), pltpu.VMEM((seq, LANES), F32),
            pltpu.VMEM((seq, LANES), F32),
        ],
        compiler_params=pltpu.CompilerParams(
            dimension_semantics=("parallel", "parallel"), vmem_limit_bytes=VMEM_LIMIT),
        name="attn_a",
    )(aq, ak, av, g_pairs)


def _attn_b_kernel(q_ref, k_ref, v_ref, g_ref, lq1_ref, lk1_ref, lq2_ref, lk2_ref, o_ref,
                   vext, s_even, s_odd, m_even, m_odd, *, seq, lambda_init):
    vext[:, :B_V_DIM] = v_ref[...]
    vext[:, B_V_DIM:] = jnp.ones((seq, B_V_DIM), BF16)

    lane = lax.broadcasted_iota(jnp.int32, (B_BLOCK_Q, LANES), 1)
    comp0 = (lane < HEAD_DIM).astype(F32)
    comp1 = 1.0 - comp0
    lam = (jnp.exp(jnp.sum(lq1_ref[...] * lk1_ref[...], axis=-1, keepdims=True))
           - jnp.exp(jnp.sum(lq2_ref[...] * lk2_ref[...], axis=-1, keepdims=True)) + lambda_init)
    gain = g_ref[...]
    n_chunks = seq // B_BLOCK_K
    n_tiles = seq // B_BLOCK_Q

    def tile_rows(t):
        return pl.ds(pl.multiple_of(t * B_BLOCK_Q, B_BLOCK_Q), B_BLOCK_Q)

    comp_rows = (slice(0, B_BLOCK_Q), slice(B_BLOCK_Q, 2 * B_BLOCK_Q))

    def stacked_q(t):
        q = q_ref[tile_rows(t), :].astype(F32)
        return ((q * comp0).astype(BF16), (q * comp1).astype(BF16))

    def score_chunk(q2, s_ref, c, mx):
        cols = slice(B_BLOCK_K * c, B_BLOCK_K * (c + 1))
        out = []
        for comp in range(2):
            s = lax.dot_general(q2[comp], k_ref[cols, :], (((1,), (1,)), ((), ())),
                                preferred_element_type=F32)
            s_ref[comp_rows[comp], cols] = s
            mxc = mx[comp]
            for j in range(B_BLOCK_K // LANES):
                mxc = jnp.maximum(mxc, s[:, LANES * j:LANES * (j + 1)])
            out.append(mxc)
        return out

    def value_chunk(s_ref, c, m, acc):
        cols = slice(B_BLOCK_K * c, B_BLOCK_K * (c + 1))
        out = []
        for comp in range(2):
            p = jnp.exp(s_ref[comp_rows[comp], cols] - m[comp_rows[comp], :]).astype(BF16)
            out.append(acc[comp] + jnp.dot(p, vext[cols, :], preferred_element_type=F32))
        return out

    def finish_tile(t, acc):
        o = [a[:, :B_V_DIM] / a[:, B_V_DIM:B_V_DIM + 1] for a in acc]
        diff = o[0] - lam * o[1]
        o_ref[tile_rows(t), :] = (_rms(diff, gain) * (1.0 - lambda_init)).astype(BF16)

    def run(t_scores, t_values, scores_to_odd):
        s_w, m_w = (s_odd, m_odd) if scores_to_odd else (s_even, m_even)
        s_r, m_r = (s_even, m_even) if scores_to_odd else (s_odd, m_odd)
        if t_scores is not None:
            q2 = stacked_q(t_scores)
            mx = [jnp.full((B_BLOCK_Q, LANES), -jnp.inf, F32)] * 2
        if t_values is not None:
            m = m_r[...]
            acc = [jnp.zeros((B_BLOCK_Q, 2 * B_V_DIM), F32)] * 2
        for c in range(n_chunks):
            if t_values is not None:
                acc = value_chunk(s_r, c, m, acc)
            if t_scores is not None:
                mx = score_chunk(q2, s_w, c, mx)
        if t_scores is not None:
            for comp in range(2):
                m_w[comp_rows[comp], :] = jnp.max(mx[comp], axis=-1, keepdims=True)
        if t_values is not None:
            finish_tile(t_values, acc)

    run(0, None, False)

    def body(i, carry):
        run(2 * i + 1, 2 * i, True)
        run(2 * i + 2, 2 * i + 1, False)
        return carry

    lax.fori_loop(0, n_tiles // 2 - 1, body, 0)
    run(n_tiles - 1, n_tiles - 2, True)
    run(None, n_tiles - 1, False)


def _attn_b(bq, bk, bv, g_out, lq1, lk1, lq2, lk2, batch, seq, lambda_init):
    n = bq.shape[0]
    blk = pl.BlockSpec((seq, LANES), lambda b, h: (b, h))
    vec = lambda w: pl.BlockSpec((1, w), lambda b, h: (0, 0))
    return pl.pallas_call(
        functools.partial(_attn_b_kernel, seq=seq, lambda_init=lambda_init),
        grid=(batch, B_HEADS),
        in_specs=[blk, blk, blk, vec(B_V_DIM),
                  vec(HEAD_DIM), vec(HEAD_DIM), vec(HEAD_DIM), vec(HEAD_DIM)],
        out_specs=blk,
        out_shape=jax.ShapeDtypeStruct((n, B_WIDTH), BF16),
        scratch_shapes=[pltpu.VMEM((seq, 2 * B_V_DIM), BF16),
                        pltpu.VMEM((2 * B_BLOCK_Q, seq), F32),
                        pltpu.VMEM((2 * B_BLOCK_Q, seq), F32),
                        pltpu.VMEM((2 * B_BLOCK_Q, 1), F32),
                        pltpu.VMEM((2 * B_BLOCK_Q, 1), F32)],
        compiler_params=pltpu.CompilerParams(
            dimension_semantics=("parallel", "parallel"), vmem_limit_bytes=VMEM_LIMIT),
        name="attn_b",
    )(bq, bk, bv, g_out, lq1, lk1, lq2, lk2)


def _out_ffn_kernel(x_ref, oa_ref, ob_ref, wo_ref, g2_ref, wgu_ref, wd_ref, gf_ref, o_ref,
                    act_ref):
    x2 = (x_ref[...]
          + jnp.dot(oa_ref[...], wo_ref[:A_WIDTH, :], preferred_element_type=F32)
          + jnp.dot(ob_ref[...], wo_ref[A_WIDTH:, :], preferred_element_type=F32))
    hb = _rms(x2, g2_ref[...]).astype(BF16)
    x3 = x2 + 0.5 * _swiglu(hb, wgu_ref, wd_ref, act_ref)
    o_ref[...] = _rms(x3, gf_ref[...])


def _out_ffn(x1, oa, ob, w_out, g2, wgu, wd, gf):
    n = x1.shape[0]
    row = lambda w: pl.BlockSpec((ROW_TILE, w), lambda i: (i, 0))
    return pl.pallas_call(
        _out_ffn_kernel,
        grid=(n // ROW_TILE,),
        in_specs=[
            row(D_MODEL), row(A_WIDTH), row(B_WIDTH),
            _const_spec((A_WIDTH + B_WIDTH, D_MODEL)),
            _const_spec((1, D_MODEL)),
            _const_spec((D_MODEL, 2 * D_FF)),
            _const_spec((D_FF, D_MODEL)),
            _const_spec((1, D_MODEL)),
        ],
        out_specs=row(D_MODEL),
        out_shape=jax.ShapeDtypeStruct((n, D_MODEL), F32),
        scratch_shapes=[pltpu.VMEM((ROW_TILE, D_FF), BF16)],
        compiler_params=pltpu.CompilerParams(
            dimension_semantics=("parallel",), vmem_limit_bytes=VMEM_LIMIT),
        name="out_ffn",
    )(x1, oa, ob, w_out, g2, wgu, wd, gf)


def _rope_tables(seq):
    inv = 1.0 / (ROPE_THETA ** (jnp.arange(0, HEAD_DIM, 2, dtype=F32) / HEAD_DIM))
    ang = jnp.arange(seq, dtype=F32)[:, None] * inv[None, :]
    ang = jnp.concatenate([ang, ang], axis=-1)
    sign = jnp.where(jnp.arange(HEAD_DIM) < HEAD_DIM // 2, -1.0, 1.0).astype(F32)
    cos = jnp.tile(jnp.cos(ang), (1, LANES // HEAD_DIM))
    sin = jnp.tile(jnp.sin(ang) * sign, (1, LANES // HEAD_DIM))
    return cos, sin


def kernel(x_prompt, x_sample, g_ffn1, w_ffn1_gu, w_ffn1_down, g_mix, w_in, g_a_q, g_a_k, g_a_out, g_b_q, g_b_k, lam_q1, lam_k1, lam_q2, lam_k2, g_b_out, w_out, g_ffn2, w_ffn2_gu, w_ffn2_down, g_final):
    depth = g_ffn1.shape[0]
    scale = HEAD_DIM ** -0.5
    head_mean = (jnp.kron(jnp.eye(A_HEADS, dtype=F32), jnp.ones((HEAD_DIM, HEAD_DIM), F32))
                 / HEAD_DIM).astype(BF16)
    row = lambda v: v.reshape(1, -1).astype(F32)
    tile_heads = lambda v: jnp.tile(v.astype(F32), A_WIDTH // HEAD_DIM).reshape(1, A_WIDTH)

    layers = []
    for l in range(depth):
        layers.append(dict(
            g1=row(g_ffn1[l]), wgu1=w_ffn1_gu[l].astype(BF16), wd1=w_ffn1_down[l].astype(BF16),
            gmix=row(g_mix[l]), w_in=w_in[l].astype(BF16),
            gaq=tile_heads(g_a_q[l]) * scale, gak=tile_heads(g_a_k[l]),
            gbq=tile_heads(g_b_q[l]) * scale, gbk=tile_heads(g_b_k[l]),
            ga_out=g_a_out[l].astype(F32).reshape(A_WIDTH // LANES, 1, LANES),
            gb_out=row(g_b_out[l]),
            lq1=row(lam_q1[l]), lk1=row(lam_k1[l]), lq2=row(lam_q2[l]), lk2=row(lam_k2[l]),
            w_out=w_out[l].astype(BF16),
            g2=row(g_ffn2[l]), wgu2=w_ffn2_gu[l].astype(BF16), wd2=w_ffn2_down[l].astype(BF16),
            gf=row(g_final[l]),
            lambda_init=0.8 - 0.6 * math.exp(-0.3 * l),
        ))

    def trunk(x):
        batch, seq, _ = x.shape
        cos, sin = _rope_tables(seq)
        x = x.reshape(batch * seq, D_MODEL)
        for p in layers:
            x1 = _ffn(x, p["g1"], p["wgu1"], p["wd1"])
            aq, ak, av, bq, bk, bv = _in_proj(x1, p["gmix"], p["w_in"], head_mean, p["gaq"],
                                              p["gak"], p["gbq"], p["gbk"], cos, sin, seq)
            oa = _attn_a(aq, ak, av, p["ga_out"], batch, seq)
            ob = _attn_b(bq, bk, bv, p["gb_out"], p["lq1"], p["lk1"], p["lq2"], p["lk2"],
                         batch, seq, p["lambda_init"])
            x = _out_ffn(x1, oa, ob, p["w_out"], p["g2"], p["wgu2"], p["wd2"], p["gf"])
        return x.reshape(batch, seq, D_MODEL)

    return (trunk(x_prompt), trunk(x_sample))
```

```python
import functools
import math

import jax
import jax.numpy as jnp
from jax import lax
from jax.experimental import pallas as pl
from jax.experimental.pallas import tpu as pltpu

F32 = jnp.float32
BF16 = jnp.bfloat16

D_MODEL = 1024
HEAD_DIM = 64
A_HEADS = 8
A_WIDTH = A_HEADS * HEAD_DIM
B_HEADS = 4
B_V_DIM = 2 * HEAD_DIM
B_WIDTH = B_HEADS * B_V_DIM
D_FF = 2816
ROPE_THETA = 10000.0
EPS = 1e-6
NEG_INF = -1e30
DILATIONS = (1, 4, 16)
HALF_WINDOW = 64

LANES = 128
FF_CHUNK = 256
N_FF_CHUNKS = D_FF // FF_CHUNK
ROW_TILE = 512
A_BLOCK_Q = 128
A_BLOCK_K = A_BLOCK_Q + 2 * HALF_WINDOW
A_UNROLL = 16
B_BLOCK_Q = 256
B_KEY_CHUNKS = 2
B_HEADS_PER_STEP = 1
VMEM_LIMIT = 56 * 1024 * 1024


def _rms(x, g):
    ms = jnp.mean(x * x, axis=-1, keepdims=True)
    return x * lax.rsqrt(ms + EPS) * g


def _swiglu(hb, wgu_ref, wd_ref, act_ref):
    for c in range(N_FF_CHUNKS):
        cols = slice(FF_CHUNK * c, FF_CHUNK * (c + 1))
        g = jnp.dot(hb, wgu_ref[:, cols], preferred_element_type=F32)
        u = jnp.dot(hb, wgu_ref[:, D_FF + FF_CHUNK * c:D_FF + FF_CHUNK * (c + 1)],
                    preferred_element_type=F32)
        act_ref[:, cols] = (g * jax.nn.sigmoid(g) * u).astype(BF16)
    return jnp.dot(act_ref[...], wd_ref[...], preferred_element_type=F32)


def _const_spec(shape):
    return pl.BlockSpec(shape, lambda *_: (0,) * len(shape), pipeline_mode=pl.Buffered(1))


def _ffn_kernel(x_ref, g_ref, wgu_ref, wd_ref, o_ref, act_ref):
    x = x_ref[...]
    hb = _rms(x, g_ref[...]).astype(BF16)
    o_ref[...] = x + 0.5 * _swiglu(hb, wgu_ref, wd_ref, act_ref)


def _ffn(x, g, wgu, wd):
    n = x.shape[0]
    return pl.pallas_call(
        _ffn_kernel,
        grid=(n // ROW_TILE,),
        in_specs=[
            pl.BlockSpec((ROW_TILE, D_MODEL), lambda i: (i, 0)),
            _const_spec((1, D_MODEL)),
            _const_spec((D_MODEL, 2 * D_FF)),
            _const_spec((D_FF, D_MODEL)),
        ],
        out_specs=pl.BlockSpec((ROW_TILE, D_MODEL), lambda i: (i, 0)),
        out_shape=jax.ShapeDtypeStruct((n, D_MODEL), F32),
        scratch_shapes=[pltpu.VMEM((ROW_TILE, D_FF), BF16)],
        compiler_params=pltpu.CompilerParams(
            dimension_semantics=("parallel",), vmem_limit_bytes=VMEM_LIMIT),
        name="ffn",
    )(x, g, wgu, wd)


def _in_proj_kernel(x_ref, g_ref, w_ref, hm_ref, gaq_ref, gak_ref, gbq_ref, gbk_ref,
                    cos_ref, sin_ref, aq_ref, ak_ref, av_ref, bq_ref, bk_ref, bv_ref):
    hb = _rms(x_ref[...], g_ref[...]).astype(BF16)
    cos = cos_ref[...]
    sin = sin_ref[...]
    lane = lax.broadcasted_iota(jnp.int32, cos.shape, 1)
    low_half = (lane & (HEAD_DIM - 1)) < (HEAD_DIM // 2)

    def seg(j):
        return jnp.dot(hb, w_ref[:, A_WIDTH * j:A_WIDTH * (j + 1)], preferred_element_type=F32)

    def norm_rope(j, gain_ref, out_ref):
        p = seg(j)
        ms = jnp.dot((p * p).astype(BF16), hm_ref[...], preferred_element_type=F32)
        y = p * lax.rsqrt(ms + EPS) * gain_ref[...]
        for c in range(A_WIDTH // LANES):
            yc = y[:, LANES * c:LANES * (c + 1)]
            rot = jnp.where(low_half, pltpu.roll(yc, LANES - HEAD_DIM // 2, 1),
                            pltpu.roll(yc, HEAD_DIM // 2, 1))
            out_ref[:, LANES * c:LANES * (c + 1)] = (yc * cos + rot * sin).astype(BF16)

    norm_rope(0, gaq_ref, aq_ref)
    norm_rope(1, gak_ref, ak_ref)
    av_ref[...] = seg(2).astype(BF16)
    norm_rope(3, gbq_ref, bq_ref)
    norm_rope(4, gbk_ref, bk_ref)
    bv_ref[...] = seg(5).astype(BF16)


def _in_proj(x1, g, w_in, head_mean, gaq, gak, gbq, gbk, cos, sin, seq):
    n = x1.shape[0]
    tiles_per_seq = seq // ROW_TILE
    row = pl.BlockSpec((ROW_TILE, A_WIDTH), lambda i: (i, 0))
    table = pl.BlockSpec((ROW_TILE, LANES), lambda i: (i % tiles_per_seq, 0))
    gain = _const_spec((1, A_WIDTH))
    out = jax.ShapeDtypeStruct((n, A_WIDTH), BF16)
    return pl.pallas_call(
        _in_proj_kernel,
        grid=(n // ROW_TILE,),
        in_specs=[
            pl.BlockSpec((ROW_TILE, D_MODEL), lambda i: (i, 0)),
            _const_spec((1, D_MODEL)),
            _const_spec((D_MODEL, 6 * A_WIDTH)),
            _const_spec((A_WIDTH, A_WIDTH)),
            gain, gain, gain, gain, table, table,
        ],
        out_specs=[row] * 6,
        out_shape=[out] * 6,
        compiler_params=pltpu.CompilerParams(
            dimension_semantics=("parallel",), vmem_limit_bytes=VMEM_LIMIT),
        name="in_proj",
    )(x1, g, w_in, head_mean, gaq, gak, gbq, gbk, cos, sin)


def _attn_a_kernel(q_ref, k_ref, v_ref, g_ref, o_ref,
                   qf, kf, vf, qg, kg, vg, qd, kd, vd, bias, m_run, l_run, acc_run, *, seq):
    lane = lax.broadcasted_iota(jnp.int32, (A_BLOCK_Q, LANES), 1)
    head0 = lane < HEAD_DIM
    qi = lax.broadcasted_iota(jnp.int32, (A_BLOCK_Q, A_BLOCK_K), 0)
    kj = lax.broadcasted_iota(jnp.int32, (A_BLOCK_Q, A_BLOCK_K), 1)
    in_band = jnp.abs(kj - HALF_WINDOW - qi) <= HALF_WINDOW
    ones_cols = jnp.ones((A_BLOCK_K, LANES), BF16)
    for variant in range(4):
        ok = in_band
        if variant & 1:
            ok = ok & (kj >= HALF_WINDOW)
        if variant & 2:
            ok = ok & (kj < A_BLOCK_Q + HALF_WINDOW)
        b = jnp.where(ok, 0.0, NEG_INF)
        bias[variant, :A_BLOCK_Q, :] = b
        bias[variant, A_BLOCK_Q:, :] = b

    copy_rows = 512
    for c in range(seq // copy_rows):
        rows = slice(copy_rows * c, copy_rows * (c + 1))
        qf[rows, :] = q_ref[rows, :].astype(F32)
        kf[rows, :] = k_ref[rows, :].astype(F32)
        vf[rows, :] = v_ref[rows, :].astype(F32)

    offset = 0
    q_offset = 0
    prev = 1
    source = (qf, kf, vf)
    for d in DILATIONS:
        run = seq // d
        padded = run + 2 * HALF_WINDOW
        chunk = min(run, 256)
        step = d // prev
        keep = (qg, kg, vg) if 1 < d < DILATIONS[-1] else None
        zeros = jnp.zeros((HALF_WINDOW, LANES), BF16)
        head0_c = lax.broadcasted_iota(jnp.int32, (chunk, LANES), 1) < HEAD_DIM
        for r in range(d):
            base = offset + r * padded
            kd[base:base + HALF_WINDOW, :] = zeros
            vd[base:base + HALF_WINDOW, :] = zeros
            kd[base + HALF_WINDOW + run:base + padded, :] = zeros
            vd[base + HALF_WINDOW + run:base + padded, :] = zeros
            for c in range(run // chunk):
                dst = base + HALF_WINDOW + c * chunk
                qdst = q_offset + r * run + c * chunk
                if d == 1:
                    src = pl.ds(c * chunk, chunk)
                    qx = q_ref[src, :].astype(F32)
                    kx = k_ref[src, :]
                    vx = v_ref[src, :]
                else:
                    src = pl.ds((r % prev) * (seq // prev) + r // prev + c * chunk * step,
                                chunk, stride=step)
                    qx, kx, vx = (ref[src, :] for ref in source)
                    if keep is not None:
                        rows = slice(r * run + c * chunk, r * run + (c + 1) * chunk)
                        for ref, x in zip(keep, (qx, kx, vx)):
                            ref[rows, :] = x
                    kx = kx.astype(BF16)
                    vx = vx.astype(BF16)
                qd[0, qdst:qdst + chunk, :] = jnp.where(head0_c, qx, 0.0).astype(BF16)
                qd[1, qdst:qdst + chunk, :] = jnp.where(head0_c, 0.0, qx).astype(BF16)
                kd[dst:dst + chunk, :] = kx
                vd[dst:dst + chunk, :] = vx
        if keep is not None:
            source = keep
        prev = d

        blocks_per_run = run // A_BLOCK_Q

        def block(i, carry, d=d, offset=offset, q_offset=q_offset,
                  blocks_per_run=blocks_per_run):
            r = i // blocks_per_run
            tb = i % blocks_per_run
            q_rows = pl.ds(pl.multiple_of(q_offset + i * A_BLOCK_Q, A_BLOCK_Q), A_BLOCK_Q)
            k_start = pl.multiple_of(offset + i * A_BLOCK_Q + r * (2 * HALF_WINDOW), A_BLOCK_Q)
            q2 = jnp.concatenate([qd[0, q_rows, :], qd[1, q_rows, :]], axis=0)
            kw = kd[pl.ds(k_start, A_BLOCK_K), :]
            vw = vd[pl.ds(k_start, A_BLOCK_K), :]
            variant = jnp.where(tb == 0, 1, 0) + jnp.where(tb == blocks_per_run - 1, 2, 0)
            s = lax.dot_general(q2, kw, (((1,), (1,)), ((), ())), preferred_element_type=F32)
            s = s + bias[variant]
            m = jnp.max(s, axis=-1, keepdims=True)
            p = jnp.exp(s - m).astype(BF16)
            pv = jnp.dot(p, jnp.concatenate([vw, ones_cols], axis=1), preferred_element_type=F32)
            m_new = jnp.where(head0, m[:A_BLOCK_Q], m[A_BLOCK_Q:])
            l_new = jnp.where(head0, pv[:A_BLOCK_Q, LANES:], pv[A_BLOCK_Q:, LANES:])
            acc_new = jnp.where(head0, pv[:A_BLOCK_Q, :LANES], pv[A_BLOCK_Q:, :LANES])
            if d == 1:
                idx = pl.ds(pl.multiple_of(i * A_BLOCK_Q, A_BLOCK_Q), A_BLOCK_Q)
                m_run[idx, :] = m_new
                l_run[idx, :] = l_new
                acc_run[idx, :] = acc_new
            else:
                idx = pl.ds(tb * (A_BLOCK_Q * d) + r, A_BLOCK_Q, stride=d)
                m_old = m_run[idx, :]
                m_tot = jnp.maximum(m_old, m_new)
                a_old = jnp.exp(m_old - m_tot)
                a_new = jnp.exp(m_new - m_tot)
                m_run[idx, :] = m_tot
                l_run[idx, :] = a_old * l_run[idx, :] + a_new * l_new
                acc_run[idx, :] = a_old * acc_run[idx, :] + a_new * acc_new
            return carry

        lax.fori_loop(0, seq // A_BLOCK_Q, block, 0, unroll=A_UNROLL)
        offset += d * padded
        q_offset += seq

    gain = g_ref[0]
    out_rows = 256
    same_head = ((lax.broadcasted_iota(jnp.int32, (LANES, LANES), 0) < HEAD_DIM)
                 == (lax.broadcasted_iota(jnp.int32, (LANES, LANES), 1) < HEAD_DIM))
    head_mean = jnp.where(same_head, 1.0 / HEAD_DIM, 0.0).astype(BF16)

    def finish(c, carry):
        rows = pl.ds(pl.multiple_of(c * out_rows, out_rows), out_rows)
        o = acc_run[rows, :] / l_run[rows, :]
        ms = jnp.dot((o * o).astype(BF16), head_mean, preferred_element_type=F32)
        o_ref[rows, :] = (o * lax.rsqrt(ms + EPS) * gain).astype(BF16)
        return carry

    lax.fori_loop(0, seq // out_rows, finish, 0, unroll=2)


def _attn_a(aq, ak, av, g_pairs, batch, seq):
    n = aq.shape[0]
    pairs = A_WIDTH // LANES
    blk = pl.BlockSpec((seq, LANES), lambda b, p: (b, p))
    kv_rows = sum(seq + 2 * HALF_WINDOW * d for d in DILATIONS)
    return pl.pallas_call(
        functools.partial(_attn_a_kernel, seq=seq),
        grid=(batch, pairs),
        in_specs=[blk, blk, blk, pl.BlockSpec((1, 1, LANES), lambda b, p: (p, 0, 0))],
        out_specs=blk,
        out_shape=jax.ShapeDtypeStruct((n, A_WIDTH), BF16),
        scratch_shapes=[pltpu.VMEM((seq, LANES), F32)] * 6 + [
            pltpu.VMEM((2, len(DILATIONS) * seq, LANES), BF16),
            pltpu.VMEM((kv_rows, LANES), BF16), pltpu.VMEM((kv_rows, LANES), BF16),
            pltpu.VMEM((4, 2 * A_BLOCK_Q, A_BLOCK_K), F32),
            pltpu.VMEM((seq, LANES), F32), pltpu.VMEM((seq, LANES), F32),
            pltpu.VMEM((seq, LANES), F32),
        ],
        compiler_params=pltpu.CompilerParams(
            dimension_semantics=("parallel", "parallel"), vmem_limit_bytes=VMEM_LIMIT),
        name="attn_a",
    )(aq, ak, av, g_pairs)


def _attn_b_kernel(q_ref, k_ref, v_ref, g_ref, lq1_ref, lk1_ref, lq2_ref, lk2_ref, o_ref,
                   vext, s_even, s_odd, m_even, m_odd, *, seq, lambda_init):
    for h in range(B_HEADS_PER_STEP):
        vext[h, :, :B_V_DIM] = v_ref[:, LANES * h:LANES * (h + 1)]
        vext[h, :, B_V_DIM:] = jnp.ones((seq, B_V_DIM), BF16)

    lane = lax.broadcasted_iota(jnp.int32, (B_BLOCK_Q, LANES), 1)
    comp0 = (lane < HEAD_DIM).astype(F32)
    comp1 = 1.0 - comp0
    lam = (jnp.exp(jnp.sum(lq1_ref[...] * lk1_ref[...], axis=-1, keepdims=True))
           - jnp.exp(jnp.sum(lq2_ref[...] * lk2_ref[...], axis=-1, keepdims=True)) + lambda_init)
    gain = g_ref[...]
    n_chunks = B_KEY_CHUNKS
    key_chunk = seq // n_chunks
    n_tiles = seq // B_BLOCK_Q

    def tile_rows(t):
        return pl.ds(pl.multiple_of(t * B_BLOCK_Q, B_BLOCK_Q), B_BLOCK_Q)

    comp_rows = (slice(0, B_BLOCK_Q), slice(B_BLOCK_Q, 2 * B_BLOCK_Q))

    def head_lanes(h):
        return slice(LANES * h, LANES * (h + 1))

    def stacked_q(h, t):
        q = q_ref[tile_rows(t), head_lanes(h)].astype(F32)
        return ((q * comp0).astype(BF16), (q * comp1).astype(BF16))

    def score_chunk(h, q2, s_ref, c, mx):
        cols = slice(key_chunk * c, key_chunk * (c + 1))
        out = []
        for comp in range(2):
            s = lax.dot_general(q2[comp], k_ref[cols, head_lanes(h)], (((1,), (1,)), ((), ())),
                                preferred_element_type=F32)
            s_ref[comp_rows[comp], cols] = s
            mxc = mx[comp]
            for j in range(key_chunk // LANES):
                mxc = jnp.maximum(mxc, s[:, LANES * j:LANES * (j + 1)])
            out.append(mxc)
        return out

    def value_chunk(h, s_ref, c, m, acc):
        cols = slice(key_chunk * c, key_chunk * (c + 1))
        out = []
        for comp in range(2):
            p = jnp.exp(s_ref[comp_rows[comp], cols] - m[comp_rows[comp], :]).astype(BF16)
            out.append(acc[comp] + jnp.dot(p, vext[h, cols, :], preferred_element_type=F32))
        return out

    def finish_tile(h, t, acc):
        o = [a[:, :B_V_DIM] / a[:, B_V_DIM:B_V_DIM + 1] for a in acc]
        diff = o[0] - lam * o[1]
        o_ref[tile_rows(t), head_lanes(h)] = (_rms(diff, gain) * (1.0 - lambda_init)).astype(BF16)

    def run(scores, values, scores_to_odd):
        s_w, m_w = (s_odd, m_odd) if scores_to_odd else (s_even, m_even)
        s_r, m_r = (s_even, m_even) if scores_to_odd else (s_odd, m_odd)
        if scores is not None:
            q2 = stacked_q(*scores)
            mx = [jnp.full((B_BLOCK_Q, LANES), -jnp.inf, F32)] * 2
        if values is not None:
            m = m_r[...]
            acc = [jnp.zeros((B_BLOCK_Q, 2 * B_V_DIM), F32)] * 2
        for c in range(n_chunks):
            if values is not None:
                acc = value_chunk(values[0], s_r, c, m, acc)
            if scores is not None:
                mx = score_chunk(scores[0], q2, s_w, c, mx)
        if scores is not None:
            for comp in range(2):
                m_w[comp_rows[comp], :] = jnp.max(mx[comp], axis=-1, keepdims=True)
        if values is not None:
            finish_tile(*values, acc)

    run((0, 0), None, False)
    for h in range(B_HEADS_PER_STEP):
        def body(i, carry, h=h):
            run((h, 2 * i + 1), (h, 2 * i), True)
            run((h, 2 * i + 2), (h, 2 * i + 1), False)
            return carry

        lax.fori_loop(0, n_tiles // 2 - 1, body, 0)
        run((h, n_tiles - 1), (h, n_tiles - 2), True)
        next_head = (h + 1, 0) if h + 1 < B_HEADS_PER_STEP else None
        run(next_head, (h, n_tiles - 1), False)


def _attn_b(bq, bk, bv, g_out, lq1, lk1, lq2, lk2, batch, seq, lambda_init):
    n = bq.shape[0]
    blk = pl.BlockSpec((seq, LANES * B_HEADS_PER_STEP), lambda b, h: (b, h))
    vec = lambda w: pl.BlockSpec((1, w), lambda b, h: (0, 0))
    return pl.pallas_call(
        functools.partial(_attn_b_kernel, seq=seq, lambda_init=lambda_init),
        grid=(batch, B_HEADS // B_HEADS_PER_STEP),
        in_specs=[blk, blk, blk, vec(B_V_DIM),
                  vec(HEAD_DIM), vec(HEAD_DIM), vec(HEAD_DIM), vec(HEAD_DIM)],
        out_specs=blk,
        out_shape=jax.ShapeDtypeStruct((n, B_WIDTH), BF16),
        scratch_shapes=[pltpu.VMEM((B_HEADS_PER_STEP, seq, 2 * B_V_DIM), BF16),
                        pltpu.VMEM((2 * B_BLOCK_Q, seq), F32),
                        pltpu.VMEM((2 * B_BLOCK_Q, seq), F32),
                        pltpu.VMEM((2 * B_BLOCK_Q, 1), F32),
                        pltpu.VMEM((2 * B_BLOCK_Q, 1), F32)],
        compiler_params=pltpu.CompilerParams(
            dimension_semantics=("parallel", "parallel"), vmem_limit_bytes=VMEM_LIMIT),
        name="attn_b",
    )(bq, bk, bv, g_out, lq1, lk1, lq2, lk2)


def _out_ffn_kernel(x_ref, oa_ref, ob_ref, wo_ref, g2_ref, wgu_ref, wd_ref, gf_ref, o_ref,
                    act_ref):
    x2 = (x_ref[...]
          + jnp.dot(oa_ref[...], wo_ref[:A_WIDTH, :], preferred_element_type=F32)
          + jnp.dot(ob_ref[...], wo_ref[A_WIDTH:, :], preferred_element_type=F32))
    hb = _rms(x2, g2_ref[...]).astype(BF16)
    x3 = x2 + 0.5 * _swiglu(hb, wgu_ref, wd_ref, act_ref)
    o_ref[...] = _rms(x3, gf_ref[...])


def _out_ffn(x1, oa, ob, w_out, g2, wgu, wd, gf):
    n = x1.shape[0]
    row = lambda w: pl.BlockSpec((ROW_TILE, w), lambda i: (i, 0))
    return pl.pallas_call(
        _out_ffn_kernel,
        grid=(n // ROW_TILE,),
        in_specs=[
            row(D_MODEL), row(A_WIDTH), row(B_WIDTH),
            _const_spec((A_WIDTH + B_WIDTH, D_MODEL)),
            _const_spec((1, D_MODEL)),
            _const_spec((D_MODEL, 2 * D_FF)),
            _const_spec((D_FF, D_MODEL)),
            _const_spec((1, D_MODEL)),
        ],
        out_specs=row(D_MODEL),
        out_shape=jax.ShapeDtypeStruct((n, D_MODEL), F32),
        scratch_shapes=[pltpu.VMEM((ROW_TILE, D_FF), BF16)],
        compiler_params=pltpu.CompilerParams(
            dimension_semantics=("parallel",), vmem_limit_bytes=VMEM_LIMIT),
        name="out_ffn",
    )(x1, oa, ob, w_out, g2, wgu, wd, gf)


def _rope_tables(seq):
    inv = 1.0 / (ROPE_THETA ** (jnp.arange(0, HEAD_DIM, 2, dtype=F32) / HEAD_DIM))
    ang = jnp.arange(seq, dtype=F32)[:, None] * inv[None, :]
    ang = jnp.concatenate([ang, ang], axis=-1)
    sign = jnp.where(jnp.arange(HEAD_DIM) < HEAD_DIM // 2, -1.0, 1.0).astype(F32)
    cos = jnp.tile(jnp.cos(ang), (1, LANES // HEAD_DIM))
    sin = jnp.tile(jnp.sin(ang) * sign, (1, LANES // HEAD_DIM))
    return cos, sin


def kernel(x_prompt, x_sample, g_ffn1, w_ffn1_gu, w_ffn1_down, g_mix, w_in, g_a_q, g_a_k, g_a_out, g_b_q, g_b_k, lam_q1, lam_k1, lam_q2, lam_k2, g_b_out, w_out, g_ffn2, w_ffn2_gu, w_ffn2_down, g_final):
    depth = g_ffn1.shape[0]
    scale = HEAD_DIM ** -0.5
    head_mean = (jnp.kron(jnp.eye(A_HEADS, dtype=F32), jnp.ones((HEAD_DIM, HEAD_DIM), F32))
                 / HEAD_DIM).astype(BF16)
    row = lambda v: v.reshape(1, -1).astype(F32)
    tile_heads = lambda v: jnp.tile(v.astype(F32), A_WIDTH // HEAD_DIM).reshape(1, A_WIDTH)

    layers = []
    for l in range(depth):
        layers.append(dict(
            g1=row(g_ffn1[l]), wgu1=w_ffn1_gu[l].astype(BF16), wd1=w_ffn1_down[l].astype(BF16),
            gmix=row(g_mix[l]), w_in=w_in[l].astype(BF16),
            gaq=tile_heads(g_a_q[l]) * scale, gak=tile_heads(g_a_k[l]),
            gbq=tile_heads(g_b_q[l]) * scale, gbk=tile_heads(g_b_k[l]),
            ga_out=g_a_out[l].astype(F32).reshape(A_WIDTH // LANES, 1, LANES),
            gb_out=row(g_b_out[l]),
            lq1=row(lam_q1[l]), lk1=row(lam_k1[l]), lq2=row(lam_q2[l]), lk2=row(lam_k2[l]),
            w_out=w_out[l].astype(BF16),
            g2=row(g_ffn2[l]), wgu2=w_ffn2_gu[l].astype(BF16), wd2=w_ffn2_down[l].astype(BF16),
            gf=row(g_final[l]),
            lambda_init=0.8 - 0.6 * math.exp(-0.3 * l),
        ))

    def trunk(x):
        batch, seq, _ = x.shape
        cos, sin = _rope_tables(seq)
        x = x.reshape(batch * seq, D_MODEL)
        for p in layers:
            x1 = _ffn(x, p["g1"], p["wgu1"], p["wd1"])
            aq, ak, av, bq, bk, bv = _in_proj(x1, p["gmix"], p["w_in"], head_mean, p["gaq"],
                                              p["gak"], p["gbq"], p["gbk"], cos, sin, seq)
            oa = _attn_a(aq, ak, av, p["ga_out"], batch, seq)
            ob = _attn_b(bq, bk, bv, p["gb_out"], p["lq1"], p["lk1"], p["lq2"], p["lk2"],
                         batch, seq, p["lambda_init"])
            x = _out_ffn(x1, oa, ob, p["w_out"], p["g2"], p["wgu2"], p["wd2"], p["gf"])
        return x.reshape(batch, seq, D_MODEL)

    return (trunk(x_prompt), trunk(x_sample))
```

```python
import functools
import math

import jax
import jax.numpy as jnp
from jax import lax
from jax.experimental import pallas as pl
from jax.experimental.pallas import tpu as pltpu

F32 = jnp.float32
BF16 = jnp.bfloat16

D_MODEL = 1024
HEAD_DIM = 64
A_HEADS = 8
A_WIDTH = A_HEADS * HEAD_DIM
B_HEADS = 4
B_V_DIM = 2 * HEAD_DIM
B_WIDTH = B_HEADS * B_V_DIM
D_FF = 2816
ROPE_THETA = 10000.0
EPS = 1e-6
NEG_INF = -1e30
DILATIONS = (1, 4, 16)
STATE_GROUP = 4
HALF_WINDOW = 64

LANES = 128
FF_CHUNK = 256
N_FF_CHUNKS = D_FF // FF_CHUNK
ROW_TILE = 512
A_BLOCK_Q = 128
A_BLOCK_K = A_BLOCK_Q + 2 * HALF_WINDOW
A_UNROLL = 16
B_BLOCK_Q = 256
B_KEY_CHUNKS = 2
B_HEADS_PER_STEP = 2
VMEM_LIMIT = 56 * 1024 * 1024


def _rms(x, g):
    ms = jnp.mean(x * x, axis=-1, keepdims=True)
    return x * lax.rsqrt(ms + EPS) * g


def _swiglu(hb, wgu_ref, wd_ref, act_ref):
    for c in range(N_FF_CHUNKS):
        cols = slice(FF_CHUNK * c, FF_CHUNK * (c + 1))
        g = jnp.dot(hb, wgu_ref[:, cols], preferred_element_type=F32)
        u = jnp.dot(hb, wgu_ref[:, D_FF + FF_CHUNK * c:D_FF + FF_CHUNK * (c + 1)],
                    preferred_element_type=F32)
        act_ref[:, cols] = (g * jax.nn.sigmoid(g) * u).astype(BF16)
    return jnp.dot(act_ref[...], wd_ref[...], preferred_element_type=F32)


def _const_spec(shape):
    return pl.BlockSpec(shape, lambda *_: (0,) * len(shape), pipeline_mode=pl.Buffered(1))


def _ffn_kernel(x_ref, g_ref, wgu_ref, wd_ref, o_ref, act_ref):
    x = x_ref[...]
    hb = _rms(x, g_ref[...]).astype(BF16)
    o_ref[...] = x + 0.5 * _swiglu(hb, wgu_ref, wd_ref, act_ref)


def _ffn(x, g, wgu, wd):
    n = x.shape[0]
    return pl.pallas_call(
        _ffn_kernel,
        grid=(n // ROW_TILE,),
        in_specs=[
            pl.BlockSpec((ROW_TILE, D_MODEL), lambda i: (i, 0)),
            _const_spec((1, D_MODEL)),
            _const_spec((D_MODEL, 2 * D_FF)),
            _const_spec((D_FF, D_MODEL)),
        ],
        out_specs=pl.BlockSpec((ROW_TILE, D_MODEL), lambda i: (i, 0)),
        out_shape=jax.ShapeDtypeStruct((n, D_MODEL), F32),
        scratch_shapes=[pltpu.VMEM((ROW_TILE, D_FF), BF16)],
        compiler_params=pltpu.CompilerParams(
            dimension_semantics=("parallel",), vmem_limit_bytes=VMEM_LIMIT),
        name="ffn",
    )(x, g, wgu, wd)


def _in_proj_kernel(x_ref, g_ref, w_ref, hm_ref, gaq_ref, gak_ref, gbq_ref, gbk_ref,
                    cos_ref, sin_ref, aq_ref, ak_ref, av_ref, bq_ref, bk_ref, bv_ref):
    hb = _rms(x_ref[...], g_ref[...]).astype(BF16)
    cos = cos_ref[...]
    sin = sin_ref[...]
    lane = lax.broadcasted_iota(jnp.int32, cos.shape, 1)
    low_half = (lane & (HEAD_DIM - 1)) < (HEAD_DIM // 2)

    def seg(j):
        return jnp.dot(hb, w_ref[:, A_WIDTH * j:A_WIDTH * (j + 1)], preferred_element_type=F32)

    def norm_rope(j, gain_ref, out_ref):
        p = seg(j)
        ms = jnp.dot((p * p).astype(BF16), hm_ref[...], preferred_element_type=F32)
        y = p * lax.rsqrt(ms + EPS) * gain_ref[...]
        for c in range(A_WIDTH // LANES):
            yc = y[:, LANES * c:LANES * (c + 1)]
            rot = jnp.where(low_half, pltpu.roll(yc, LANES - HEAD_DIM // 2, 1),
                            pltpu.roll(yc, HEAD_DIM // 2, 1))
            out_ref[:, LANES * c:LANES * (c + 1)] = (yc * cos + rot * sin).astype(BF16)

    norm_rope(0, gaq_ref, aq_ref)
    norm_rope(1, gak_ref, ak_ref)
    av_ref[...] = seg(2).astype(BF16)
    norm_rope(3, gbq_ref, bq_ref)
    norm_rope(4, gbk_ref, bk_ref)
    bv_ref[...] = seg(5).astype(BF16)


def _in_proj(x1, g, w_in, head_mean, gaq, gak, gbq, gbk, cos, sin, seq):
    n = x1.shape[0]
    tiles_per_seq = seq // ROW_TILE
    row = pl.BlockSpec((ROW_TILE, A_WIDTH), lambda i: (i, 0))
    table = pl.BlockSpec((ROW_TILE, LANES), lambda i: (i % tiles_per_seq, 0))
    gain = _const_spec((1, A_WIDTH))
    out = jax.ShapeDtypeStruct((n, A_WIDTH), BF16)
    return pl.pallas_call(
        _in_proj_kernel,
        grid=(n // ROW_TILE,),
        in_specs=[
            pl.BlockSpec((ROW_TILE, D_MODEL), lambda i: (i, 0)),
            _const_spec((1, D_MODEL)),
            _const_spec((D_MODEL, 6 * A_WIDTH)),
            _const_spec((A_WIDTH, A_WIDTH)),
            gain, gain, gain, gain, table, table,
        ],
        out_specs=[row] * 6,
        out_shape=[out] * 6,
        compiler_params=pltpu.CompilerParams(
            dimension_semantics=("parallel",), vmem_limit_bytes=VMEM_LIMIT),
        name="in_proj",
    )(x1, g, w_in, head_mean, gaq, gak, gbq, gbk, cos, sin)


def _attn_a_kernel(q_ref, k_ref, v_ref, g_ref, o_ref,
                   qf, kf, vf, qg, kg, vg, qd, kd, vd, bias, m_run, l_run, acc_run, *, seq):
    lane = lax.broadcasted_iota(jnp.int32, (A_BLOCK_Q, LANES), 1)
    head0 = lane < HEAD_DIM
    qi = lax.broadcasted_iota(jnp.int32, (A_BLOCK_Q, A_BLOCK_K), 0)
    kj = lax.broadcasted_iota(jnp.int32, (A_BLOCK_Q, A_BLOCK_K), 1)
    in_band = jnp.abs(kj - HALF_WINDOW - qi) <= HALF_WINDOW
    ones_cols = jnp.ones((A_BLOCK_K, LANES), BF16)
    for variant in range(4):
        ok = in_band
        if variant & 1:
            ok = ok & (kj >= HALF_WINDOW)
        if variant & 2:
            ok = ok & (kj < A_BLOCK_Q + HALF_WINDOW)
        b = jnp.where(ok, 0.0, NEG_INF)
        bias[variant, :A_BLOCK_Q, :] = b
        bias[variant, A_BLOCK_Q:, :] = b

    copy_rows = 512
    for c in range(seq // copy_rows):
        rows = slice(copy_rows * c, copy_rows * (c + 1))
        qf[rows, :] = q_ref[rows, :].astype(F32)
        kf[rows, :] = k_ref[rows, :].astype(F32)
        vf[rows, :] = v_ref[rows, :].astype(F32)

    offset = 0
    q_offset = 0
    layout = {}
    prev = 1
    source = (qf, kf, vf)
    for d in DILATIONS:
        run = seq // d
        padded = run + 2 * HALF_WINDOW
        chunk = min(run, 256)
        step = d // prev
        keep = (qg, kg, vg) if 1 < d < DILATIONS[-1] else None
        zeros = jnp.zeros((HALF_WINDOW, LANES), BF16)
        head0_c = lax.broadcasted_iota(jnp.int32, (chunk, LANES), 1) < HEAD_DIM
        for r in range(d):
            base = offset + r * padded
            kd[base:base + HALF_WINDOW, :] = zeros
            vd[base:base + HALF_WINDOW, :] = zeros
            kd[base + HALF_WINDOW + run:base + padded, :] = zeros
            vd[base + HALF_WINDOW + run:base + padded, :] = zeros
            for c in range(run // chunk):
                dst = base + HALF_WINDOW + c * chunk
                qdst = q_offset + r * run + c * chunk
                if d == 1:
                    src = pl.ds(c * chunk, chunk)
                    qx = q_ref[src, :].astype(F32)
                    kx = k_ref[src, :]
                    vx = v_ref[src, :]
                else:
                    src = pl.ds((r % prev) * (seq // prev) + r // prev + c * chunk * step,
                                chunk, stride=step)
                    qx, kx, vx = (ref[src, :] for ref in source)
                    if keep is not None:
                        rows = slice(r * run + c * chunk, r * run + (c + 1) * chunk)
                        for ref, x in zip(keep, (qx, kx, vx)):
                            ref[rows, :] = x
                    kx = kx.astype(BF16)
                    vx = vx.astype(BF16)
                qd[0, qdst:qdst + chunk, :] = jnp.where(head0_c, qx, 0.0).astype(BF16)
                qd[1, qdst:qdst + chunk, :] = jnp.where(head0_c, 0.0, qx).astype(BF16)
                kd[dst:dst + chunk, :] = kx
                vd[dst:dst + chunk, :] = vx
        if keep is not None:
            source = keep
        prev = d
        layout[d] = (offset, q_offset)
        offset += d * padded
        q_offset += seq

    group_run = seq // STATE_GROUP
    first = (qf, kf, vf)
    state = (m_run, l_run, acc_run)

    for d in DILATIONS:
        offset, q_offset = layout[d]
        blocks_per_run = seq // d // A_BLOCK_Q

        def block(i, carry, d=d, offset=offset, q_offset=q_offset,
                  blocks_per_run=blocks_per_run):
            r = i // blocks_per_run
            tb = i % blocks_per_run
            q_rows = pl.ds(pl.multiple_of(q_offset + i * A_BLOCK_Q, A_BLOCK_Q), A_BLOCK_Q)
            k_start = pl.multiple_of(offset + i * A_BLOCK_Q + r * (2 * HALF_WINDOW), A_BLOCK_Q)
            q2 = jnp.concatenate([qd[0, q_rows, :], qd[1, q_rows, :]], axis=0)
            kw = kd[pl.ds(k_start, A_BLOCK_K), :]
            vw = vd[pl.ds(k_start, A_BLOCK_K), :]
            variant = jnp.where(tb == 0, 1, 0) + jnp.where(tb == blocks_per_run - 1, 2, 0)
            s = lax.dot_general(q2, kw, (((1,), (1,)), ((), ())), preferred_element_type=F32)
            s = s + bias[variant]
            m = jnp.max(s, axis=-1, keepdims=True)
            p = jnp.exp(s - m).astype(BF16)
            pv = jnp.dot(p, jnp.concatenate([vw, ones_cols], axis=1), preferred_element_type=F32)
            m_new = jnp.where(head0, m[:A_BLOCK_Q], m[A_BLOCK_Q:])
            l_new = jnp.where(head0, pv[:A_BLOCK_Q, LANES:], pv[A_BLOCK_Q:, LANES:])
            acc_new = jnp.where(head0, pv[:A_BLOCK_Q, :LANES], pv[A_BLOCK_Q:, :LANES])
            if d == 1:
                idx = pl.ds(pl.multiple_of(i * A_BLOCK_Q, A_BLOCK_Q), A_BLOCK_Q)
                for ref, x in zip(first, (m_new, l_new, acc_new)):
                    ref[idx, :] = x
            else:
                if d == STATE_GROUP:
                    idx = pl.ds(pl.multiple_of(i * A_BLOCK_Q, A_BLOCK_Q), A_BLOCK_Q)
                else:
                    step = d // STATE_GROUP
                    idx = pl.ds((r % STATE_GROUP) * group_run + r // STATE_GROUP
                                + tb * (A_BLOCK_Q * step), A_BLOCK_Q, stride=step)
                m_old = m_run[idx, :]
                m_tot = jnp.maximum(m_old, m_new)
                a_old = jnp.exp(m_old - m_tot)
                a_new = jnp.exp(m_new - m_tot)
                m_run[idx, :] = m_tot
                l_run[idx, :] = a_old * l_run[idx, :] + a_new * l_new
                acc_run[idx, :] = a_old * acc_run[idx, :] + a_new * acc_new
            return carry

        lax.fori_loop(0, seq // A_BLOCK_Q, block, 0, unroll=A_UNROLL)

        if d == 1:
            chunk = 256
            for g in range(STATE_GROUP):
                for c in range(group_run // chunk):
                    src = pl.ds(g + c * chunk * STATE_GROUP, chunk, stride=STATE_GROUP)
                    dst = slice(g * group_run + c * chunk, g * group_run + (c + 1) * chunk)
                    for ref, tmp in zip(state, first):
                        ref[dst, :] = tmp[src, :]

    gain = g_ref[0]
    out_rows = 256
    chunks_per_group = group_run // out_rows
    out32 = qf
    same_head = ((lax.broadcasted_iota(jnp.int32, (LANES, LANES), 0) < HEAD_DIM)
                 == (lax.broadcasted_iota(jnp.int32, (LANES, LANES), 1) < HEAD_DIM))
    head_mean = jnp.where(same_head, 1.0 / HEAD_DIM, 0.0).astype(BF16)

    def finish(c, carry):
        rows = pl.ds(pl.multiple_of(c * out_rows, out_rows), out_rows)
        o = acc_run[rows, :] / l_run[rows, :]
        ms = jnp.dot((o * o).astype(BF16), head_mean, preferred_element_type=F32)
        g = c // chunks_per_group
        t0 = (c % chunks_per_group) * out_rows
        out32[pl.ds(g + t0 * STATE_GROUP, out_rows, stride=STATE_GROUP), :] = (
            o * lax.rsqrt(ms + EPS) * gain)
        return carry

    lax.fori_loop(0, seq // out_rows, finish, 0, unroll=2)
    copy_rows = 512
    for c in range(seq // copy_rows):
        rows = slice(copy_rows * c, copy_rows * (c + 1))
        o_ref[rows, :] = out32[rows, :].astype(BF16)


def _attn_a(aq, ak, av, g_pairs, batch, seq):
    n = aq.shape[0]
    pairs = A_WIDTH // LANES
    blk = pl.BlockSpec((seq, LANES), lambda b, p: (b, p))
    kv_rows = sum(seq + 2 * HALF_WINDOW * d for d in DILATIONS)
    return pl.pallas_call(
        functools.partial(_attn_a_kernel, seq=seq),
        grid=(batch, pairs),
        in_specs=[blk, blk, blk, pl.BlockSpec((1, 1, LANES), lambda b, p: (p, 0, 0))],
        out_specs=blk,
        out_shape=jax.ShapeDtypeStruct((n, A_WIDTH), BF16),
        scratch_shapes=[pltpu.VMEM((seq, LANES), F32)] * 6 + [
            pltpu.VMEM((2, len(DILATIONS) * seq, LANES), BF16),
            pltpu.VMEM((kv_rows, LANES), BF16), pltpu.VMEM((kv_rows, LANES), BF16),
            pltpu.VMEM((4, 2 * A_BLOCK_Q, A_BLOCK_K), F32),
            pltpu.VMEM((seq, LANES), F32), pltpu.VMEM((seq, LANES), F32),
            pltpu.VMEM((seq, LANES), F32),
        ],
        compiler_params=pltpu.CompilerParams(
            dimension_semantics=("parallel", "parallel"), vmem_limit_bytes=VMEM_LIMIT),
        name="attn_a",
    )(aq, ak, av, g_pairs)


def _attn_b_kernel(q_ref, k_ref, v_ref, g_ref, lq1_ref, lk1_ref, lq2_ref, lk2_ref, o_ref,
                   vext, s_even, s_odd, m_even, m_odd, *, seq, lambda_init):
    for h in range(B_HEADS_PER_STEP):
        vext[h, :, :B_V_DIM] = v_ref[:, LANES * h:LANES * (h + 1)]
        vext[h, :, B_V_DIM:] = jnp.ones((seq, B_V_DIM), BF16)

    lane = lax.broadcasted_iota(jnp.int32, (B_BLOCK_Q, LANES), 1)
    comp0 = (lane < HEAD_DIM).astype(F32)
    comp1 = 1.0 - comp0
    lam = (jnp.exp(jnp.sum(lq1_ref[...] * lk1_ref[...], axis=-1, keepdims=True))
           - jnp.exp(jnp.sum(lq2_ref[...] * lk2_ref[...], axis=-1, keepdims=True)) + lambda_init)
    gain = g_ref[...]
    n_chunks = B_KEY_CHUNKS
    key_chunk = seq // n_chunks
    n_tiles = seq // B_BLOCK_Q

    def tile_rows(t):
        return pl.ds(pl.multiple_of(t * B_BLOCK_Q, B_BLOCK_Q), B_BLOCK_Q)

    comp_rows = (slice(0, B_BLOCK_Q), slice(B_BLOCK_Q, 2 * B_BLOCK_Q))

    def head_lanes(h):
        return slice(LANES * h, LANES * (h + 1))

    def stacked_q(h, t):
        q = q_ref[tile_rows(t), head_lanes(h)].astype(F32)
        return ((q * comp0).astype(BF16), (q * comp1).astype(BF16))

    def score_chunk(h, q2, s_ref, c, mx):
        cols = slice(key_chunk * c, key_chunk * (c + 1))
        out = []
        for comp in range(2):
            s = lax.dot_general(q2[comp], k_ref[cols, head_lanes(h)], (((1,), (1,)), ((), ())),
                                preferred_element_type=F32)
            s_ref[comp_rows[comp], cols] = s
            mxc = mx[comp]
            for j in range(key_chunk // LANES):
                mxc = jnp.maximum(mxc, s[:, LANES * j:LANES * (j + 1)])
            out.append(mxc)
        return out

    def value_chunk(h, s_ref, c, m, acc):
        cols = slice(key_chunk * c, key_chunk * (c + 1))
        out = []
        for comp in range(2):
            p = jnp.exp(s_ref[comp_rows[comp], cols] - m[comp_rows[comp], :]).astype(BF16)
            out.append(acc[comp] + jnp.dot(p, vext[h, cols, :], preferred_element_type=F32))
        return out

    def finish_tile(h, t, acc):
        o = [a[:, :B_V_DIM] / a[:, B_V_DIM:B_V_DIM + 1] for a in acc]
        diff = o[0] - lam * o[1]
        o_ref[tile_rows(t), head_lanes(h)] = (_rms(diff, gain) * (1.0 - lambda_init)).astype(BF16)

    def run(scores, values, scores_to_odd):
        s_w, m_w = (s_odd, m_odd) if scores_to_odd else (s_even, m_even)
        s_r, m_r = (s_even, m_even) if scores_to_odd else (s_odd, m_odd)
        if scores is not None:
            q2 = stacked_q(*scores)
            mx = [jnp.full((B_BLOCK_Q, LANES), -jnp.inf, F32)] * 2
        if values is not None:
            m = m_r[...]
            acc = [jnp.zeros((B_BLOCK_Q, 2 * B_V_DIM), F32)] * 2
        for c in range(n_chunks):
            if values is not None:
                acc = value_chunk(values[0], s_r, c, m, acc)
            if scores is not None:
                mx = score_chunk(scores[0], q2, s_w, c, mx)
        if scores is not None:
            for comp in range(2):
                m_w[comp_rows[comp], :] = jnp.max(mx[comp], axis=-1, keepdims=True)
        if values is not None:
            finish_tile(*values, acc)

    run((0, 0), None, False)
    for h in range(B_HEADS_PER_STEP):
        def body(i, carry, h=h):
            run((h, 2 * i + 1), (h, 2 * i), True)
            run((h, 2 * i + 2), (h, 2 * i + 1), False)
            return carry

        lax.fori_loop(0, n_tiles // 2 - 1, body, 0)
        run((h, n_tiles - 1), (h, n_tiles - 2), True)
        next_head = (h + 1, 0) if h + 1 < B_HEADS_PER_STEP else None
        run(next_head, (h, n_tiles - 1), False)


def _attn_b(bq, bk, bv, g_out, lq1, lk1, lq2, lk2, batch, seq, lambda_init):
    n = bq.shape[0]
    blk = pl.BlockSpec((seq, LANES * B_HEADS_PER_STEP), lambda b, h: (b, h))
    vec = lambda w: pl.BlockSpec((1, w), lambda b, h: (0, 0))
    return pl.pallas_call(
        functools.partial(_attn_b_kernel, seq=seq, lambda_init=lambda_init),
        grid=(batch, B_HEADS // B_HEADS_PER_STEP),
        in_specs=[blk, blk, blk, vec(B_V_DIM),
                  vec(HEAD_DIM), vec(HEAD_DIM), vec(HEAD_DIM), vec(HEAD_DIM)],
        out_specs=blk,
        out_shape=jax.ShapeDtypeStruct((n, B_WIDTH), BF16),
        scratch_shapes=[pltpu.VMEM((B_HEADS_PER_STEP, seq, 2 * B_V_DIM), BF16),
                        pltpu.VMEM((2 * B_BLOCK_Q, seq), F32),
                        pltpu.VMEM((2 * B_BLOCK_Q, seq), F32),
                        pltpu.VMEM((2 * B_BLOCK_Q, 1), F32),
                        pltpu.VMEM((2 * B_BLOCK_Q, 1), F32)],
        compiler_params=pltpu.CompilerParams(
            dimension_semantics=("parallel", "parallel"), vmem_limit_bytes=VMEM_LIMIT),
        name="attn_b",
    )(bq, bk, bv, g_out, lq1, lk1, lq2, lk2)


def _out_ffn_kernel(x_ref, oa_ref, ob_ref, wo_ref, g2_ref, wgu_ref, wd_ref, gf_ref, o_ref,
                    act_ref):
    x2 = (x_ref[...]
          + jnp.dot(oa_ref[...], wo_ref[:A_WIDTH, :], preferred_element_type=F32)
          + jnp.dot(ob_ref[...], wo_ref[A_WIDTH:, :], preferred_element_type=F32))
    hb = _rms(x2, g2_ref[...]).astype(BF16)
    x3 = x2 + 0.5 * _swiglu(hb, wgu_ref, wd_ref, act_ref)
    o_ref[...] = _rms(x3, gf_ref[...])


def _out_ffn(x1, oa, ob, w_out, g2, wgu, wd, gf):
    n = x1.shape[0]
    row = lambda w: pl.BlockSpec((ROW_TILE, w), lambda i: (i, 0))
    return pl.pallas_call(
        _out_ffn_kernel,
        grid=(n // ROW_TILE,),
        in_specs=[
            row(D_MODEL), row(A_WIDTH), row(B_WIDTH),
            _const_spec((A_WIDTH + B_WIDTH, D_MODEL)),
            _const_spec((1, D_MODEL)),
            _const_spec((D_MODEL, 2 * D_FF)),
            _const_spec((D_FF, D_MODEL)),
            _const_spec((1, D_MODEL)),
        ],
        out_specs=row(D_MODEL),
        out_shape=jax.ShapeDtypeStruct((n, D_MODEL), F32),
        scratch_shapes=[pltpu.VMEM((ROW_TILE, D_FF), BF16)],
        compiler_params=pltpu.CompilerParams(
            dimension_semantics=("parallel",), vmem_limit_bytes=VMEM_LIMIT),
        name="out_ffn",
    )(x1, oa, ob, w_out, g2, wgu, wd, gf)


def _rope_tables(seq):
    inv = 1.0 / (ROPE_THETA ** (jnp.arange(0, HEAD_DIM, 2, dtype=F32) / HEAD_DIM))
    ang = jnp.arange(seq, dtype=F32)[:, None] * inv[None, :]
    ang = jnp.concatenate([ang, ang], axis=-1)
    sign = jnp.where(jnp.arange(HEAD_DIM) < HEAD_DIM // 2, -1.0, 1.0).astype(F32)
    cos = jnp.tile(jnp.cos(ang), (1, LANES // HEAD_DIM))
    sin = jnp.tile(jnp.sin(ang) * sign, (1, LANES // HEAD_DIM))
    return cos, sin


def kernel(x_prompt, x_sample, g_ffn1, w_ffn1_gu, w_ffn1_down, g_mix, w_in, g_a_q, g_a_k, g_a_out, g_b_q, g_b_k, lam_q1, lam_k1, lam_q2, lam_k2, g_b_out, w_out, g_ffn2, w_ffn2_gu, w_ffn2_down, g_final):
    depth = g_ffn1.shape[0]
    scale = HEAD_DIM ** -0.5
    head_mean = (jnp.kron(jnp.eye(A_HEADS, dtype=F32), jnp.ones((HEAD_DIM, HEAD_DIM), F32))
                 / HEAD_DIM).astype(BF16)
    row = lambda v: v.reshape(1, -1).astype(F32)
    tile_heads = lambda v: jnp.tile(v.astype(F32), A_WIDTH // HEAD_DIM).reshape(1, A_WIDTH)

    layers = []
    for l in range(depth):
        layers.append(dict(
            g1=row(g_ffn1[l]), wgu1=w_ffn1_gu[l].astype(BF16), wd1=w_ffn1_down[l].astype(BF16),
            gmix=row(g_mix[l]), w_in=w_in[l].astype(BF16),
            gaq=tile_heads(g_a_q[l]) * scale, gak=tile_heads(g_a_k[l]),
            gbq=tile_heads(g_b_q[l]) * scale, gbk=tile_heads(g_b_k[l]),
            ga_out=g_a_out[l].astype(F32).reshape(A_WIDTH // LANES, 1, LANES),
            gb_out=row(g_b_out[l]),
            lq1=row(lam_q1[l]), lk1=row(lam_k1[l]), lq2=row(lam_q2[l]), lk2=row(lam_k2[l]),
            w_out=w_out[l].astype(BF16),
            g2=row(g_ffn2[l]), wgu2=w_ffn2_gu[l].astype(BF16), wd2=w_ffn2_down[l].astype(BF16),
            gf=row(g_final[l]),
            lambda_init=0.8 - 0.6 * math.exp(-0.3 * l),
        ))

    def trunk(x):
        batch, seq, _ = x.shape
        cos, sin = _rope_tables(seq)
        x = x.reshape(batch * seq, D_MODEL)
        for p in layers:
            x1 = _ffn(x, p["g1"], p["wgu1"], p["wd1"])
            aq, ak, av, bq, bk, bv = _in_proj(x1, p["gmix"], p["w_in"], head_mean, p["gaq"],
                                              p["gak"], p["gbq"], p["gbk"], cos, sin, seq)
            oa = _attn_a(aq, ak, av, p["ga_out"], batch, seq)
            ob = _attn_b(bq, bk, bv, p["gb_out"], p["lq1"], p["lk1"], p["lq2"], p["lk2"],
                         batch, seq, p["lambda_init"])
            x = _out_ffn(x1, oa, ob, p["w_out"], p["g2"], p["wgu2"], p["wd2"], p["gf"])
        return x.reshape(batch, seq, D_MODEL)

    return (trunk(x_prompt), trunk(x_sample))
```

```python
import functools
import math

import jax
import jax.numpy as jnp
from jax import lax
from jax.experimental import pallas as pl
from jax.experimental.pallas import tpu as pltpu

F32 = jnp.float32
BF16 = jnp.bfloat16

D_MODEL = 1024
HEAD_DIM = 64
A_HEADS = 8
A_WIDTH = A_HEADS * HEAD_DIM
B_HEADS = 4
B_V_DIM = 2 * HEAD_DIM
B_WIDTH = B_HEADS * B_V_DIM
D_FF = 2816
ROPE_THETA = 10000.0
EPS = 1e-6
NEG_INF = -1e30
DILATIONS = (1, 4, 16)
STATE_GROUP = 4
HALF_WINDOW = 64

LANES = 128
FF_CHUNK = 256
N_FF_CHUNKS = D_FF // FF_CHUNK
ROW_TILE = 1024
A_BLOCK_Q = 128
A_BLOCK_K = A_BLOCK_Q + 2 * HALF_WINDOW
A_UNROLL = 16
B_BLOCK_Q = 256
B_KEY_CHUNKS = 2
B_HEADS_PER_STEP = 1
VMEM_LIMIT = 56 * 1024 * 1024


def _rms(x, g):
    ms = jnp.mean(x * x, axis=-1, keepdims=True)
    return x * lax.rsqrt(ms + EPS) * g


def _swiglu(hb, wgu_ref, wd_ref, act_ref):
    for c in range(N_FF_CHUNKS):
        cols = slice(FF_CHUNK * c, FF_CHUNK * (c + 1))
        g = jnp.dot(hb, wgu_ref[:, cols], preferred_element_type=F32)
        u = jnp.dot(hb, wgu_ref[:, D_FF + FF_CHUNK * c:D_FF + FF_CHUNK * (c + 1)],
                    preferred_element_type=F32)
        act_ref[:, cols] = (g * jax.nn.sigmoid(g) * u).astype(BF16)
    return jnp.dot(act_ref[...], wd_ref[...], preferred_element_type=F32)


def _const_spec(shape):
    return pl.BlockSpec(shape, lambda *_: (0,) * len(shape), pipeline_mode=pl.Buffered(1))


def _ffn_kernel(x_ref, g_ref, wgu_ref, wd_ref, o_ref, act_ref):
    x = x_ref[...]
    hb = _rms(x, g_ref[...]).astype(BF16)
    o_ref[...] = x + 0.5 * _swiglu(hb, wgu_ref, wd_ref, act_ref)


def _ffn(x, g, wgu, wd):
    n = x.shape[0]
    return pl.pallas_call(
        _ffn_kernel,
        grid=(n // ROW_TILE,),
        in_specs=[
            pl.BlockSpec((ROW_TILE, D_MODEL), lambda i: (i, 0)),
            _const_spec((1, D_MODEL)),
            _const_spec((D_MODEL, 2 * D_FF)),
            _const_spec((D_FF, D_MODEL)),
        ],
        out_specs=pl.BlockSpec((ROW_TILE, D_MODEL), lambda i: (i, 0)),
        out_shape=jax.ShapeDtypeStruct((n, D_MODEL), F32),
        scratch_shapes=[pltpu.VMEM((ROW_TILE, D_FF), BF16)],
        compiler_params=pltpu.CompilerParams(
            dimension_semantics=("parallel",), vmem_limit_bytes=VMEM_LIMIT),
        name="ffn",
    )(x, g, wgu, wd)


def _in_proj_kernel(x_ref, g_ref, w_ref, hm_ref, gaq_ref, gak_ref, gbq_ref, gbk_ref,
                    cos_ref, sin_ref, aq_ref, ak_ref, av_ref, bq_ref, bk_ref, bv_ref):
    hb = _rms(x_ref[...], g_ref[...]).astype(BF16)
    cos = cos_ref[...]
    sin = sin_ref[...]
    lane = lax.broadcasted_iota(jnp.int32, cos.shape, 1)
    low_half = (lane & (HEAD_DIM - 1)) < (HEAD_DIM // 2)

    def seg(j):
        return jnp.dot(hb, w_ref[:, A_WIDTH * j:A_WIDTH * (j + 1)], preferred_element_type=F32)

    def norm_rope(j, gain_ref, out_ref):
        p = seg(j)
        ms = jnp.dot((p * p).astype(BF16), hm_ref[...], preferred_element_type=F32)
        y = p * lax.rsqrt(ms + EPS) * gain_ref[...]
        for c in range(A_WIDTH // LANES):
            yc = y[:, LANES * c:LANES * (c + 1)]
            rot = jnp.where(low_half, pltpu.roll(yc, LANES - HEAD_DIM // 2, 1),
                            pltpu.roll(yc, HEAD_DIM // 2, 1))
            out_ref[:, LANES * c:LANES * (c + 1)] = (yc * cos + rot * sin).astype(BF16)

    norm_rope(0, gaq_ref, aq_ref)
    norm_rope(1, gak_ref, ak_ref)
    av_ref[...] = seg(2).astype(BF16)
    norm_rope(3, gbq_ref, bq_ref)
    norm_rope(4, gbk_ref, bk_ref)
    bv_ref[...] = seg(5).astype(BF16)


def _in_proj(x1, g, w_in, head_mean, gaq, gak, gbq, gbk, cos, sin, seq):
    n = x1.shape[0]
    tiles_per_seq = seq // ROW_TILE
    row = pl.BlockSpec((ROW_TILE, A_WIDTH), lambda i: (i, 0))
    table = pl.BlockSpec((ROW_TILE, LANES), lambda i: (i % tiles_per_seq, 0))
    gain = _const_spec((1, A_WIDTH))
    out = jax.ShapeDtypeStruct((n, A_WIDTH), BF16)
    return pl.pallas_call(
        _in_proj_kernel,
        grid=(n // ROW_TILE,),
        in_specs=[
            pl.BlockSpec((ROW_TILE, D_MODEL), lambda i: (i, 0)),
            _const_spec((1, D_MODEL)),
            _const_spec((D_MODEL, 6 * A_WIDTH)),
            _const_spec((A_WIDTH, A_WIDTH)),
            gain, gain, gain, gain, table, table,
        ],
        out_specs=[row] * 6,
        out_shape=[out] * 6,
        compiler_params=pltpu.CompilerParams(
            dimension_semantics=("parallel",), vmem_limit_bytes=VMEM_LIMIT),
        name="in_proj",
    )(x1, g, w_in, head_mean, gaq, gak, gbq, gbk, cos, sin)


def _attn_a_kernel(q_ref, k_ref, v_ref, g_ref, o_ref,
                   qf, kf, vf, qg, kg, vg, qd, kd, vd, bias, m_run, l_run, acc_run, *, seq):
    lane = lax.broadcasted_iota(jnp.int32, (A_BLOCK_Q, LANES), 1)
    head0 = lane < HEAD_DIM
    qi = lax.broadcasted_iota(jnp.int32, (A_BLOCK_Q, A_BLOCK_K), 0)
    kj = lax.broadcasted_iota(jnp.int32, (A_BLOCK_Q, A_BLOCK_K), 1)
    in_band = jnp.abs(kj - HALF_WINDOW - qi) <= HALF_WINDOW
    ones_cols = jnp.ones((A_BLOCK_K, LANES), BF16)
    for variant in range(4):
        ok = in_band
        if variant & 1:
            ok = ok & (kj >= HALF_WINDOW)
        if variant & 2:
            ok = ok & (kj < A_BLOCK_Q + HALF_WINDOW)
        b = jnp.where(ok, 0.0, NEG_INF)
        bias[variant, :A_BLOCK_Q, :] = b
        bias[variant, A_BLOCK_Q:, :] = b

    copy_rows = 512
    for c in range(seq // copy_rows):
        rows = slice(copy_rows * c, copy_rows * (c + 1))
        qf[rows, :] = q_ref[rows, :].astype(F32)
        kf[rows, :] = k_ref[rows, :].astype(F32)
        vf[rows, :] = v_ref[rows, :].astype(F32)

    offset = 0
    q_offset = 0
    layout = {}
    prev = 1
    source = (qf, kf, vf)
    for d in DILATIONS:
        run = seq // d
        padded = run + 2 * HALF_WINDOW
        chunk = min(run, 256)
        step = d // prev
        keep = (qg, kg, vg) if 1 < d < DILATIONS[-1] else None
        zeros = jnp.zeros((HALF_WINDOW, LANES), BF16)
        head0_c = lax.broadcasted_iota(jnp.int32, (chunk, LANES), 1) < HEAD_DIM
        for r in range(d):
            base = offset + r * padded
            kd[base:base + HALF_WINDOW, :] = zeros
            vd[base:base + HALF_WINDOW, :] = zeros
            kd[base + HALF_WINDOW + run:base + padded, :] = zeros
            vd[base + HALF_WINDOW + run:base + padded, :] = zeros
            for c in range(run // chunk):
                dst = base + HALF_WINDOW + c * chunk
                qdst = q_offset + r * run + c * chunk
                if d == 1:
                    src = pl.ds(c * chunk, chunk)
                    qx = q_ref[src, :].astype(F32)
                    kx = k_ref[src, :]
                    vx = v_ref[src, :]
                else:
                    src = pl.ds((r % prev) * (seq // prev) + r // prev + c * chunk * step,
                                chunk, stride=step)
                    qx, kx, vx = (ref[src, :] for ref in source)
                    if keep is not None:
                        rows = slice(r * run + c * chunk, r * run + (c + 1) * chunk)
                        for ref, x in zip(keep, (qx, kx, vx)):
                            ref[rows, :] = x
                    kx = kx.astype(BF16)
                    vx = vx.astype(BF16)
                qd[0, qdst:qdst + chunk, :] = jnp.where(head0_c, qx, 0.0).astype(BF16)
                qd[1, qdst:qdst + chunk, :] = jnp.where(head0_c, 0.0, qx).astype(BF16)
                kd[dst:dst + chunk, :] = kx
                vd[dst:dst + chunk, :] = vx
        if keep is not None:
            source = keep
        prev = d
        layout[d] = (offset, q_offset)
        offset += d * padded
        q_offset += seq

    group_run = seq // STATE_GROUP
    first = (qf, kf, vf)
    state = (m_run, l_run, acc_run)

    for d in DILATIONS:
        offset, q_offset = layout[d]
        blocks_per_run = seq // d // A_BLOCK_Q

        def block(i, carry, d=d, offset=offset, q_offset=q_offset,
                  blocks_per_run=blocks_per_run):
            r = i // blocks_per_run
            tb = i % blocks_per_run
            q_rows = pl.ds(pl.multiple_of(q_offset + i * A_BLOCK_Q, A_BLOCK_Q), A_BLOCK_Q)
            k_start = pl.multiple_of(offset + i * A_BLOCK_Q + r * (2 * HALF_WINDOW), A_BLOCK_Q)
            q2 = jnp.concatenate([qd[0, q_rows, :], qd[1, q_rows, :]], axis=0)
            kw = kd[pl.ds(k_start, A_BLOCK_K), :]
            vw = vd[pl.ds(k_start, A_BLOCK_K), :]
            variant = jnp.where(tb == 0, 1, 0) + jnp.where(tb == blocks_per_run - 1, 2, 0)
            s = lax.dot_general(q2, kw, (((1,), (1,)), ((), ())), preferred_element_type=F32)
            s = s + bias[variant]
            m = jnp.max(s, axis=-1, keepdims=True)
            p = jnp.exp(s - m).astype(BF16)
            pv = jnp.dot(p, jnp.concatenate([vw, ones_cols], axis=1), preferred_element_type=F32)
            m_new = jnp.where(head0, m[:A_BLOCK_Q], m[A_BLOCK_Q:])
            l_new = jnp.where(head0, pv[:A_BLOCK_Q, LANES:], pv[A_BLOCK_Q:, LANES:])
            acc_new = jnp.where(head0, pv[:A_BLOCK_Q, :LANES], pv[A_BLOCK_Q:, :LANES])
            if d == 1:
                idx = pl.ds(pl.multiple_of(i * A_BLOCK_Q, A_BLOCK_Q), A_BLOCK_Q)
                for ref, x in zip(first, (m_new, l_new, acc_new)):
                    ref[idx, :] = x
            else:
                if d == STATE_GROUP:
                    idx = pl.ds(pl.multiple_of(i * A_BLOCK_Q, A_BLOCK_Q), A_BLOCK_Q)
                else:
                    step = d // STATE_GROUP
                    idx = pl.ds((r % STATE_GROUP) * group_run + r // STATE_GROUP
                                + tb * (A_BLOCK_Q * step), A_BLOCK_Q, stride=step)
                m_old = m_run[idx, :]
                m_tot = jnp.maximum(m_old, m_new)
                a_old = jnp.exp(m_old - m_tot)
                a_new = jnp.exp(m_new - m_tot)
                m_run[idx, :] = m_tot
                l_run[idx, :] = a_old * l_run[idx, :] + a_new * l_new
                acc_run[idx, :] = a_old * acc_run[idx, :] + a_new * acc_new
            return carry

        lax.fori_loop(0, seq // A_BLOCK_Q, block, 0, unroll=A_UNROLL)

        if d == 1:
            chunk = 256
            for g in range(STATE_GROUP):
                for c in range(group_run // chunk):
                    src = pl.ds(g + c * chunk * STATE_GROUP, chunk, stride=STATE_GROUP)
                    dst = slice(g * group_run + c * chunk, g * group_run + (c + 1) * chunk)
                    for ref, tmp in zip(state, first):
                        ref[dst, :] = tmp[src, :]

    gain = g_ref[0]
    out_rows = 256
    chunks_per_group = group_run // out_rows
    out32 = qf
    same_head = ((lax.broadcasted_iota(jnp.int32, (LANES, LANES), 0) < HEAD_DIM)
                 == (lax.broadcasted_iota(jnp.int32, (LANES, LANES), 1) < HEAD_DIM))
    head_mean = jnp.where(same_head, 1.0 / HEAD_DIM, 0.0).astype(BF16)

    def finish(c, carry):
        rows = pl.ds(pl.multiple_of(c * out_rows, out_rows), out_rows)
        o = acc_run[rows, :] / l_run[rows, :]
        ms = jnp.dot((o * o).astype(BF16), head_mean, preferred_element_type=F32)
        g = c // chunks_per_group
        t0 = (c % chunks_per_group) * out_rows
        out32[pl.ds(g + t0 * STATE_GROUP, out_rows, stride=STATE_GROUP), :] = (
            o * lax.rsqrt(ms + EPS) * gain)
        return carry

    lax.fori_loop(0, seq // out_rows, finish, 0, unroll=2)
    copy_rows = 512
    for c in range(seq // copy_rows):
        rows = slice(copy_rows * c, copy_rows * (c + 1))
        o_ref[rows, :] = out32[rows, :].astype(BF16)


def _attn_a(aq, ak, av, g_pairs, batch, seq):
    n = aq.shape[0]
    pairs = A_WIDTH // LANES
    blk = pl.BlockSpec((seq, LANES), lambda b, p: (b, p))
    kv_rows = sum(seq + 2 * HALF_WINDOW * d for d in DILATIONS)
    return pl.pallas_call(
        functools.partial(_attn_a_kernel, seq=seq),
        grid=(batch, pairs),
        in_specs=[blk, blk, blk, pl.BlockSpec((1, 1, LANES), lambda b, p: (p, 0, 0))],
        out_specs=blk,
        out_shape=jax.ShapeDtypeStruct((n, A_WIDTH), BF16),
        scratch_shapes=[pltpu.VMEM((seq, LANES), F32)] * 6 + [
            pltpu.VMEM((2, len(DILATIONS) * seq, LANES), BF16),
            pltpu.VMEM((kv_rows, LANES), BF16), pltpu.VMEM((kv_rows, LANES), BF16),
            pltpu.VMEM((4, 2 * A_BLOCK_Q, A_BLOCK_K), F32),
            pltpu.VMEM((seq, LANES), F32), pltpu.VMEM((seq, LANES), F32),
            pltpu.VMEM((seq, LANES), F32),
        ],
        compiler_params=pltpu.CompilerParams(
            dimension_semantics=("parallel", "parallel"), vmem_limit_bytes=VMEM_LIMIT),
        name="attn_a",
    )(aq, ak, av, g_pairs)


def _attn_b_kernel(q_ref, k_ref, v_ref, g_ref, lq1_ref, lk1_ref, lq2_ref, lk2_ref, o_ref,
                   vext, s_even, s_odd, m_even, m_odd, *, seq, lambda_init):
    for h in range(B_HEADS_PER_STEP):
        vext[h, :, :B_V_DIM] = v_ref[:, LANES * h:LANES * (h + 1)]
        vext[h, :, B_V_DIM:] = jnp.ones((seq, B_V_DIM), BF16)

    lane = lax.broadcasted_iota(jnp.int32, (B_BLOCK_Q, LANES), 1)
    comp0 = (lane < HEAD_DIM).astype(F32)
    comp1 = 1.0 - comp0
    lam = (jnp.exp(jnp.sum(lq1_ref[...] * lk1_ref[...], axis=-1, keepdims=True))
           - jnp.exp(jnp.sum(lq2_ref[...] * lk2_ref[...], axis=-1, keepdims=True)) + lambda_init)
    gain = g_ref[...]
    n_chunks = B_KEY_CHUNKS
    key_chunk = seq // n_chunks
    n_tiles = seq // B_BLOCK_Q

    def tile_rows(t):
        return pl.ds(pl.multiple_of(t * B_BLOCK_Q, B_BLOCK_Q), B_BLOCK_Q)

    comp_rows = (slice(0, B_BLOCK_Q), slice(B_BLOCK_Q, 2 * B_BLOCK_Q))

    def head_lanes(h):
        return slice(LANES * h, LANES * (h + 1))

    def stacked_q(h, t):
        q = q_ref[tile_rows(t), head_lanes(h)].astype(F32)
        return ((q * comp0).astype(BF16), (q * comp1).astype(BF16))

    def score_chunk(h, q2, s_ref, c, mx):
        cols = slice(key_chunk * c, key_chunk * (c + 1))
        out = []
        for comp in range(2):
            s = lax.dot_general(q2[comp], k_ref[cols, head_lanes(h)], (((1,), (1,)), ((), ())),
                                preferred_element_type=F32)
            s_ref[comp_rows[comp], cols] = s
            mxc = mx[comp]
            for j in range(key_chunk // LANES):
                mxc = jnp.maximum(mxc, s[:, LANES * j:LANES * (j + 1)])
            out.append(mxc)
        return out

    def value_chunk(h, s_ref, c, m, acc):
        cols = slice(key_chunk * c, key_chunk * (c + 1))
        out = []
        for comp in range(2):
            p = jnp.exp(s_ref[comp_rows[comp], cols] - m[comp_rows[comp], :]).astype(BF16)
            out.append(acc[comp] + jnp.dot(p, vext[h, cols, :], preferred_element_type=F32))
        return out

    def finish_tile(h, t, acc):
        o = [a[:, :B_V_DIM] / a[:, B_V_DIM:B_V_DIM + 1] for a in acc]
        diff = o[0] - lam * o[1]
        o_ref[tile_rows(t), head_lanes(h)] = (_rms(diff, gain) * (1.0 - lambda_init)).astype(BF16)

    def run(scores, values, scores_to_odd):
        s_w, m_w = (s_odd, m_odd) if scores_to_odd else (s_even, m_even)
        s_r, m_r = (s_even, m_even) if scores_to_odd else (s_odd, m_odd)
        if scores is not None:
            q2 = stacked_q(*scores)
            mx = [jnp.full((B_BLOCK_Q, LANES), -jnp.inf, F32)] * 2
        if values is not None:
            m = m_r[...]
            acc = [jnp.zeros((B_BLOCK_Q, 2 * B_V_DIM), F32)] * 2
        for c in range(n_chunks):
            if values is not None:
                acc = value_chunk(values[0], s_r, c, m, acc)
            if scores is not None:
                mx = score_chunk(scores[0], q2, s_w, c, mx)
        if scores is not None:
            for comp in range(2):
                m_w[comp_rows[comp], :] = jnp.max(mx[comp], axis=-1, keepdims=True)
        if values is not None:
            finish_tile(*values, acc)

    run((0, 0), None, False)
    for h in range(B_HEADS_PER_STEP):
        def body(i, carry, h=h):
            run((h, 2 * i + 1), (h, 2 * i), True)
            run((h, 2 * i + 2), (h, 2 * i + 1), False)
            return carry

        lax.fori_loop(0, n_tiles // 2 - 1, body, 0)
        run((h, n_tiles - 1), (h, n_tiles - 2), True)
        next_head = (h + 1, 0) if h + 1 < B_HEADS_PER_STEP else None
        run(next_head, (h, n_tiles - 1), False)


def _attn_b(bq, bk, bv, g_out, lq1, lk1, lq2, lk2, batch, seq, lambda_init):
    n = bq.shape[0]
    blk = pl.BlockSpec((seq, LANES * B_HEADS_PER_STEP), lambda b, h: (b, h))
    vec = lambda w: pl.BlockSpec((1, w), lambda b, h: (0, 0))
    return pl.pallas_call(
        functools.partial(_attn_b_kernel, seq=seq, lambda_init=lambda_init),
        grid=(batch, B_HEADS // B_HEADS_PER_STEP),
        in_specs=[blk, blk, blk, vec(B_V_DIM),
                  vec(HEAD_DIM), vec(HEAD_DIM), vec(HEAD_DIM), vec(HEAD_DIM)],
        out_specs=blk,
        out_shape=jax.ShapeDtypeStruct((n, B_WIDTH), BF16),
        scratch_shapes=[pltpu.VMEM((B_HEADS_PER_STEP, seq, 2 * B_V_DIM), BF16),
                        pltpu.VMEM((2 * B_BLOCK_Q, seq), F32),
                        pltpu.VMEM((2 * B_BLOCK_Q, seq), F32),
                        pltpu.VMEM((2 * B_BLOCK_Q, 1), F32),
                        pltpu.VMEM((2 * B_BLOCK_Q, 1), F32)],
        compiler_params=pltpu.CompilerParams(
            dimension_semantics=("parallel", "parallel"), vmem_limit_bytes=VMEM_LIMIT),
        name="attn_b",
    )(bq, bk, bv, g_out, lq1, lk1, lq2, lk2)


def _out_ffn_kernel(x_ref, oa_ref, ob_ref, wo_ref, g2_ref, wgu_ref, wd_ref, gf_ref, o_ref,
                    act_ref):
    x2 = (x_ref[...]
          + jnp.dot(oa_ref[...], wo_ref[:A_WIDTH, :], preferred_element_type=F32)
          + jnp.dot(ob_ref[...], wo_ref[A_WIDTH:, :], preferred_element_type=F32))
    hb = _rms(x2, g2_ref[...]).astype(BF16)
    x3 = x2 + 0.5 * _swiglu(hb, wgu_ref, wd_ref, act_ref)
    o_ref[...] = _rms(x3, gf_ref[...])


def _out_ffn(x1, oa, ob, w_out, g2, wgu, wd, gf):
    n = x1.shape[0]
    row = lambda w: pl.BlockSpec((ROW_TILE, w), lambda i: (i, 0))
    return pl.pallas_call(
        _out_ffn_kernel,
        grid=(n // ROW_TILE,),
        in_specs=[
            row(D_MODEL), row(A_WIDTH), row(B_WIDTH),
            _const_spec((A_WIDTH + B_WIDTH, D_MODEL)),
            _const_spec((1, D_MODEL)),
            _const_spec((D_MODEL, 2 * D_FF)),
            _const_spec((D_FF, D_MODEL)),
            _const_spec((1, D_MODEL)),
        ],
        out_specs=row(D_MODEL),
        out_shape=jax.ShapeDtypeStruct((n, D_MODEL), F32),
        scratch_shapes=[pltpu.VMEM((ROW_TILE, D_FF), BF16)],
        compiler_params=pltpu.CompilerParams(
            dimension_semantics=("parallel",), vmem_limit_bytes=VMEM_LIMIT),
        name="out_ffn",
    )(x1, oa, ob, w_out, g2, wgu, wd, gf)


def _rope_tables(seq):
    inv = 1.0 / (ROPE_THETA ** (jnp.arange(0, HEAD_DIM, 2, dtype=F32) / HEAD_DIM))
    ang = jnp.arange(seq, dtype=F32)[:, None] * inv[None, :]
    ang = jnp.concatenate([ang, ang], axis=-1)
    sign = jnp.where(jnp.arange(HEAD_DIM) < HEAD_DIM // 2, -1.0, 1.0).astype(F32)
    cos = jnp.tile(jnp.cos(ang), (1, LANES // HEAD_DIM))
    sin = jnp.tile(jnp.sin(ang) * sign, (1, LANES // HEAD_DIM))
    return cos, sin


def kernel(x_prompt, x_sample, g_ffn1, w_ffn1_gu, w_ffn1_down, g_mix, w_in, g_a_q, g_a_k, g_a_out, g_b_q, g_b_k, lam_q1, lam_k1, lam_q2, lam_k2, g_b_out, w_out, g_ffn2, w_ffn2_gu, w_ffn2_down, g_final):
    depth = g_ffn1.shape[0]
    scale = HEAD_DIM ** -0.5
    head_mean = (jnp.kron(jnp.eye(A_HEADS, dtype=F32), jnp.ones((HEAD_DIM, HEAD_DIM), F32))
                 / HEAD_DIM).astype(BF16)
    row = lambda v: v.reshape(1, -1).astype(F32)
    tile_heads = lambda v: jnp.tile(v.astype(F32), A_WIDTH // HEAD_DIM).reshape(1, A_WIDTH)

    layers = []
    for l in range(depth):
        layers.append(dict(
            g1=row(g_ffn1[l]), wgu1=w_ffn1_gu[l].astype(BF16), wd1=w_ffn1_down[l].astype(BF16),
            gmix=row(g_mix[l]), w_in=w_in[l].astype(BF16),
            gaq=tile_heads(g_a_q[l]) * scale, gak=tile_heads(g_a_k[l]),
            gbq=tile_heads(g_b_q[l]) * scale, gbk=tile_heads(g_b_k[l]),
            ga_out=g_a_out[l].astype(F32).reshape(A_WIDTH // LANES, 1, LANES),
            gb_out=row(g_b_out[l]),
            lq1=row(lam_q1[l]), lk1=row(lam_k1[l]), lq2=row(lam_q2[l]), lk2=row(lam_k2[l]),
            w_out=w_out[l].astype(BF16),
            g2=row(g_ffn2[l]), wgu2=w_ffn2_gu[l].astype(BF16), wd2=w_ffn2_down[l].astype(BF16),
            gf=row(g_final[l]),
            lambda_init=0.8 - 0.6 * math.exp(-0.3 * l),
        ))

    def trunk(x):
        batch, seq, _ = x.shape
        cos, sin = _rope_tables(seq)
        x = x.reshape(batch * seq, D_MODEL)
        for p in layers:
            x1 = _ffn(x, p["g1"], p["wgu1"], p["wd1"])
            aq, ak, av, bq, bk, bv = _in_proj(x1, p["gmix"], p["w_in"], head_mean, p["gaq"],
                                              p["gak"], p["gbq"], p["gbk"], cos, sin, seq)
            oa = _attn_a(aq, ak, av, p["ga_out"], batch, seq)
            ob = _attn_b(bq, bk, bv, p["gb_out"], p["lq1"], p["lk1"], p["lq2"], p["lk2"],
                         batch, seq, p["lambda_init"])
            x = _out_ffn(x1, oa, ob, p["w_out"], p["g2"], p["wgu2"], p["wd2"], p["gf"])
        return x.reshape(batch, seq, D_MODEL)

    return (trunk(x_prompt), trunk(x_sample))
```

```python
import functools
import math

import jax
import jax.numpy as jnp
from jax import lax
from jax.experimental import pallas as pl
from jax.experimental.pallas import tpu as pltpu

F32 = jnp.float32
BF16 = jnp.bfloat16

D_MODEL = 1024
HEAD_DIM = 64
A_HEADS = 8
A_WIDTH = A_HEADS * HEAD_DIM
B_HEADS = 4
B_V_DIM = 2 * HEAD_DIM
B_WIDTH = B_HEADS * B_V_DIM
D_FF = 2816
ROPE_THETA = 10000.0
EPS = 1e-6
NEG_INF = -1e30
DILATIONS = (1, 4, 16)
STATE_GROUP = 4
HALF_WINDOW = 64

LANES = 128
MXU_WIDTH = 256
FF_CHUNK = 256
N_FF_CHUNKS = D_FF // FF_CHUNK
ROW_TILE = 1024
A_BLOCK_Q = 128
A_BLOCK_K = A_BLOCK_Q + 2 * HALF_WINDOW
A_UNROLL = 16
B_BLOCK_Q = 256
B_KEY_CHUNKS = 2
B_HEADS_PER_STEP = 1
VMEM_LIMIT = 56 * 1024 * 1024


def _rms(x, g):
    ms = jnp.mean(x * x, axis=-1, keepdims=True)
    return x * lax.rsqrt(ms + EPS) * g


def _swiglu(hb, wgu_ref, wd_ref, act_ref):
    for c in range(N_FF_CHUNKS):
        cols = slice(FF_CHUNK * c, FF_CHUNK * (c + 1))
        g = jnp.dot(hb, wgu_ref[:, cols], preferred_element_type=F32)
        u = jnp.dot(hb, wgu_ref[:, D_FF + FF_CHUNK * c:D_FF + FF_CHUNK * (c + 1)],
                    preferred_element_type=F32)
        act_ref[:, cols] = (g * jax.nn.sigmoid(g) * u).astype(BF16)
    return jnp.dot(act_ref[...], wd_ref[...], preferred_element_type=F32)


def _const_spec(shape):
    return pl.BlockSpec(shape, lambda *_: (0,) * len(shape), pipeline_mode=pl.Buffered(1))


def _ffn_kernel(x_ref, g_ref, wgu_ref, wd_ref, o_ref, act_ref):
    x = x_ref[...]
    hb = _rms(x, g_ref[...]).astype(BF16)
    o_ref[...] = x + 0.5 * _swiglu(hb, wgu_ref, wd_ref, act_ref)


def _ffn(x, g, wgu, wd):
    n = x.shape[0]
    return pl.pallas_call(
        _ffn_kernel,
        grid=(n // ROW_TILE,),
        in_specs=[
            pl.BlockSpec((ROW_TILE, D_MODEL), lambda i: (i, 0)),
            _const_spec((1, D_MODEL)),
            _const_spec((D_MODEL, 2 * D_FF)),
            _const_spec((D_FF, D_MODEL)),
        ],
        out_specs=pl.BlockSpec((ROW_TILE, D_MODEL), lambda i: (i, 0)),
        out_shape=jax.ShapeDtypeStruct((n, D_MODEL), F32),
        scratch_shapes=[pltpu.VMEM((ROW_TILE, D_FF), BF16)],
        compiler_params=pltpu.CompilerParams(
            dimension_semantics=("parallel",), vmem_limit_bytes=VMEM_LIMIT),
        name="ffn",
    )(x, g, wgu, wd)


def _in_proj_kernel(x_ref, g_ref, w_ref, hm_ref, gaq_ref, gak_ref, gbq_ref, gbk_ref,
                    cos_ref, sin_ref, aq_ref, ak_ref, av_ref, bq_ref, bk_ref, bv_ref):
    hb = _rms(x_ref[...], g_ref[...]).astype(BF16)
    cos = cos_ref[...]
    sin = sin_ref[...]
    lane = lax.broadcasted_iota(jnp.int32, cos.shape, 1)
    low_half = (lane & (HEAD_DIM - 1)) < (HEAD_DIM // 2)

    def seg(j):
        return jnp.dot(hb, w_ref[:, A_WIDTH * j:A_WIDTH * (j + 1)], preferred_element_type=F32)

    def norm_rope(j, gain_ref, out_ref):
        p = seg(j)
        sq = (p * p).astype(BF16)
        ms = jnp.concatenate(
            [jnp.dot(sq[:, MXU_WIDTH * c:MXU_WIDTH * (c + 1)], hm_ref[...],
                     preferred_element_type=F32) for c in range(A_WIDTH // MXU_WIDTH)], axis=1)
        y = p * lax.rsqrt(ms + EPS) * gain_ref[...]
        for c in range(A_WIDTH // LANES):
            yc = y[:, LANES * c:LANES * (c + 1)]
            rot = jnp.where(low_half, pltpu.roll(yc, LANES - HEAD_DIM // 2, 1),
                            pltpu.roll(yc, HEAD_DIM // 2, 1))
            out_ref[:, LANES * c:LANES * (c + 1)] = (yc * cos + rot * sin).astype(BF16)

    norm_rope(0, gaq_ref, aq_ref)
    norm_rope(1, gak_ref, ak_ref)
    av_ref[...] = seg(2).astype(BF16)
    norm_rope(3, gbq_ref, bq_ref)
    norm_rope(4, gbk_ref, bk_ref)
    bv_ref[...] = seg(5).astype(BF16)


def _in_proj(x1, g, w_in, head_mean, gaq, gak, gbq, gbk, cos, sin, seq):
    n = x1.shape[0]
    tiles_per_seq = seq // ROW_TILE
    row = pl.BlockSpec((ROW_TILE, A_WIDTH), lambda i: (i, 0))
    table = pl.BlockSpec((ROW_TILE, LANES), lambda i: (i % tiles_per_seq, 0))
    gain = _const_spec((1, A_WIDTH))
    out = jax.ShapeDtypeStruct((n, A_WIDTH), BF16)
    return pl.pallas_call(
        _in_proj_kernel,
        grid=(n // ROW_TILE,),
        in_specs=[
            pl.BlockSpec((ROW_TILE, D_MODEL), lambda i: (i, 0)),
            _const_spec((1, D_MODEL)),
            _const_spec((D_MODEL, 6 * A_WIDTH)),
            _const_spec((MXU_WIDTH, MXU_WIDTH)),
            gain, gain, gain, gain, table, table,
        ],
        out_specs=[row] * 6,
        out_shape=[out] * 6,
        compiler_params=pltpu.CompilerParams(
            dimension_semantics=("parallel",), vmem_limit_bytes=VMEM_LIMIT),
        name="in_proj",
    )(x1, g, w_in, head_mean, gaq, gak, gbq, gbk, cos, sin)


def _attn_a_kernel(q_ref, k_ref, v_ref, g_ref, o_ref,
                   qf, kf, vf, qg, kg, vg, qd, kd, vd, bias, m_run, l_run, acc_run, *, seq):
    lane = lax.broadcasted_iota(jnp.int32, (A_BLOCK_Q, LANES), 1)
    head0 = lane < HEAD_DIM
    qi = lax.broadcasted_iota(jnp.int32, (A_BLOCK_Q, A_BLOCK_K), 0)
    kj = lax.broadcasted_iota(jnp.int32, (A_BLOCK_Q, A_BLOCK_K), 1)
    in_band = jnp.abs(kj - HALF_WINDOW - qi) <= HALF_WINDOW
    ones_cols = jnp.ones((A_BLOCK_K, LANES), BF16)
    for variant in range(4):
        ok = in_band
        if variant & 1:
            ok = ok & (kj >= HALF_WINDOW)
        if variant & 2:
            ok = ok & (kj < A_BLOCK_Q + HALF_WINDOW)
        b = jnp.where(ok, 0.0, NEG_INF)
        bias[variant, :A_BLOCK_Q, :] = b
        bias[variant, A_BLOCK_Q:, :] = b

    copy_rows = 512
    for c in range(seq // copy_rows):
        rows = slice(copy_rows * c, copy_rows * (c + 1))
        qf[rows, :] = q_ref[rows, :].astype(F32)
        kf[rows, :] = k_ref[rows, :].astype(F32)
        vf[rows, :] = v_ref[rows, :].astype(F32)

    offset = 0
    q_offset = 0
    layout = {}
    prev = 1
    source = (qf, kf, vf)
    for d in DILATIONS:
        run = seq // d
        padded = run + 2 * HALF_WINDOW
        chunk = min(run, 256)
        step = d // prev
        keep = (qg, kg, vg) if 1 < d < DILATIONS[-1] else None
        zeros = jnp.zeros((HALF_WINDOW, LANES), BF16)
        head0_c = lax.broadcasted_iota(jnp.int32, (chunk, LANES), 1) < HEAD_DIM
        for r in range(d):
            base = offset + r * padded
            kd[base:base + HALF_WINDOW, :] = zeros
            vd[base:base + HALF_WINDOW, :] = zeros
            kd[base + HALF_WINDOW + run:base + padded, :] = zeros
            vd[base + HALF_WINDOW + run:base + padded, :] = zeros
            for c in range(run // chunk):
                dst = base + HALF_WINDOW + c * chunk
                qdst = q_offset + r * run + c * chunk
                if d == 1:
                    src = pl.ds(c * chunk, chunk)
                    qx = q_ref[src, :].astype(F32)
                    kx = k_ref[src, :]
                    vx = v_ref[src, :]
                else:
                    src = pl.ds((r % prev) * (seq // prev) + r // prev + c * chunk * step,
                                chunk, stride=step)
                    qx, kx, vx = (ref[src, :] for ref in source)
                    if keep is not None:
                        rows = slice(r * run + c * chunk, r * run + (c + 1) * chunk)
                        for ref, x in zip(keep, (qx, kx, vx)):
                            ref[rows, :] = x
                    kx = kx.astype(BF16)
                    vx = vx.astype(BF16)
                qd[0, qdst:qdst + chunk, :] = jnp.where(head0_c, qx, 0.0).astype(BF16)
                qd[1, qdst:qdst + chunk, :] = jnp.where(head0_c, 0.0, qx).astype(BF16)
                kd[dst:dst + chunk, :] = kx
                vd[dst:dst + chunk, :] = vx
        if keep is not None:
            source = keep
        prev = d
        layout[d] = (offset, q_offset)
        offset += d * padded
        q_offset += seq

    group_run = seq // STATE_GROUP
    first = (qf, kf, vf)
    state = (m_run, l_run, acc_run)

    for d in DILATIONS:
        offset, q_offset = layout[d]
        blocks_per_run = seq // d // A_BLOCK_Q

        def block(i, carry, d=d, offset=offset, q_offset=q_offset,
                  blocks_per_run=blocks_per_run):
            r = i // blocks_per_run
            tb = i % blocks_per_run
            q_rows = pl.ds(pl.multiple_of(q_offset + i * A_BLOCK_Q, A_BLOCK_Q), A_BLOCK_Q)
            k_start = pl.multiple_of(offset + i * A_BLOCK_Q + r * (2 * HALF_WINDOW), A_BLOCK_Q)
            q2 = jnp.concatenate([qd[0, q_rows, :], qd[1, q_rows, :]], axis=0)
            kw = kd[pl.ds(k_start, A_BLOCK_K), :]
            vw = vd[pl.ds(k_start, A_BLOCK_K), :]
            variant = jnp.where(tb == 0, 1, 0) + jnp.where(tb == blocks_per_run - 1, 2, 0)
            s = lax.dot_general(q2, kw, (((1,), (1,)), ((), ())), preferred_element_type=F32)
            s = s + bias[variant]
            m = jnp.max(s, axis=-1, keepdims=True)
            p = jnp.exp(s - m).astype(BF16)
            pv = jnp.dot(p, jnp.concatenate([vw, ones_cols], axis=1), preferred_element_type=F32)
            m_new = jnp.where(head0, m[:A_BLOCK_Q], m[A_BLOCK_Q:])
            l_new = jnp.where(head0, pv[:A_BLOCK_Q, LANES:], pv[A_BLOCK_Q:, LANES:])
            acc_new = jnp.where(head0, pv[:A_BLOCK_Q, :LANES], pv[A_BLOCK_Q:, :LANES])
            if d == 1:
                idx = pl.ds(pl.multiple_of(i * A_BLOCK_Q, A_BLOCK_Q), A_BLOCK_Q)
                for ref, x in zip(first, (m_new, l_new, acc_new)):
                    ref[idx, :] = x
            else:
                if d == STATE_GROUP:
                    idx = pl.ds(pl.multiple_of(i * A_BLOCK_Q, A_BLOCK_Q), A_BLOCK_Q)
                else:
                    step = d // STATE_GROUP
                    idx = pl.ds((r % STATE_GROUP) * group_run + r // STATE_GROUP
                                + tb * (A_BLOCK_Q * step), A_BLOCK_Q, stride=step)
                m_old = m_run[idx, :]
                m_tot = jnp.maximum(m_old, m_new)
                a_old = jnp.exp(m_old - m_tot)
                a_new = jnp.exp(m_new - m_tot)
                m_run[idx, :] = m_tot
                l_run[idx, :] = a_old * l_run[idx, :] + a_new * l_new
                acc_run[idx, :] = a_old * acc_run[idx, :] + a_new * acc_new
            return carry

        lax.fori_loop(0, seq // A_BLOCK_Q, block, 0, unroll=A_UNROLL)

        if d == 1:
            chunk = 256
            for g in range(STATE_GROUP):
                for c in range(group_run // chunk):
                    src = pl.ds(g + c * chunk * STATE_GROUP, chunk, stride=STATE_GROUP)
                    dst = slice(g * group_run + c * chunk, g * group_run + (c + 1) * chunk)
                    for ref, tmp in zip(state, first):
                        ref[dst, :] = tmp[src, :]

    gain = g_ref[0]
    out_rows = 256
    chunks_per_group = group_run // out_rows
    out32 = qf
    same_head = ((lax.broadcasted_iota(jnp.int32, (LANES, LANES), 0) < HEAD_DIM)
                 == (lax.broadcasted_iota(jnp.int32, (LANES, LANES), 1) < HEAD_DIM))
    head_mean = jnp.where(same_head, 1.0 / HEAD_DIM, 0.0).astype(BF16)

    def finish(c, carry):
        rows = pl.ds(pl.multiple_of(c * out_rows, out_rows), out_rows)
        o = acc_run[rows, :] / l_run[rows, :]
        ms = jnp.dot((o * o).astype(BF16), head_mean, preferred_element_type=F32)
        g = c // chunks_per_group
        t0 = (c % chunks_per_group) * out_rows
        out32[pl.ds(g + t0 * STATE_GROUP, out_rows, stride=STATE_GROUP), :] = (
            o * lax.rsqrt(ms + EPS) * gain)
        return carry

    lax.fori_loop(0, seq // out_rows, finish, 0, unroll=2)
    copy_rows = 512
    for c in range(seq // copy_rows):
        rows = slice(copy_rows * c, copy_rows * (c + 1))
        o_ref[rows, :] = out32[rows, :].astype(BF16)


def _attn_a(aq, ak, av, g_pairs, batch, seq):
    n = aq.shape[0]
    pairs = A_WIDTH // LANES
    blk = pl.BlockSpec((seq, LANES), lambda b, p: (b, p))
    kv_rows = sum(seq + 2 * HALF_WINDOW * d for d in DILATIONS)
    return pl.pallas_call(
        functools.partial(_attn_a_kernel, seq=seq),
        grid=(batch, pairs),
        in_specs=[blk, blk, blk, pl.BlockSpec((1, 1, LANES), lambda b, p: (p, 0, 0))],
        out_specs=blk,
        out_shape=jax.ShapeDtypeStruct((n, A_WIDTH), BF16),
        scratch_shapes=[pltpu.VMEM((seq, LANES), F32)] * 6 + [
            pltpu.VMEM((2, len(DILATIONS) * seq, LANES), BF16),
            pltpu.VMEM((kv_rows, LANES), BF16), pltpu.VMEM((kv_rows, LANES), BF16),
            pltpu.VMEM((4, 2 * A_BLOCK_Q, A_BLOCK_K), F32),
            pltpu.VMEM((seq, LANES), F32), pltpu.VMEM((seq, LANES), F32),
            pltpu.VMEM((seq, LANES), F32),
        ],
        compiler_params=pltpu.CompilerParams(
            dimension_semantics=("parallel", "parallel"), vmem_limit_bytes=VMEM_LIMIT),
        name="attn_a",
    )(aq, ak, av, g_pairs)


def _attn_b_kernel(q_ref, k_ref, v_ref, g_ref, lq1_ref, lk1_ref, lq2_ref, lk2_ref, o_ref,
                   vext, s_even, s_odd, m_even, m_odd, *, seq, lambda_init):
    for h in range(B_HEADS_PER_STEP):
        vext[h, :, :B_V_DIM] = v_ref[:, LANES * h:LANES * (h + 1)]
        vext[h, :, B_V_DIM:] = jnp.ones((seq, B_V_DIM), BF16)

    lane = lax.broadcasted_iota(jnp.int32, (B_BLOCK_Q, LANES), 1)
    comp0 = (lane < HEAD_DIM).astype(F32)
    comp1 = 1.0 - comp0
    lam = (jnp.exp(jnp.sum(lq1_ref[...] * lk1_ref[...], axis=-1, keepdims=True))
           - jnp.exp(jnp.sum(lq2_ref[...] * lk2_ref[...], axis=-1, keepdims=True)) + lambda_init)
    gain = g_ref[...]
    n_chunks = B_KEY_CHUNKS
    key_chunk = seq // n_chunks
    n_tiles = seq // B_BLOCK_Q

    def tile_rows(t):
        return pl.ds(pl.multiple_of(t * B_BLOCK_Q, B_BLOCK_Q), B_BLOCK_Q)

    comp_rows = (slice(0, B_BLOCK_Q), slice(B_BLOCK_Q, 2 * B_BLOCK_Q))

    def head_lanes(h):
        return slice(LANES * h, LANES * (h + 1))

    def stacked_q(h, t):
        q = q_ref[tile_rows(t), head_lanes(h)].astype(F32)
        return ((q * comp0).astype(BF16), (q * comp1).astype(BF16))

    def score_chunk(h, q2, s_ref, c, mx):
        cols = slice(key_chunk * c, key_chunk * (c + 1))
        out = []
        for comp in range(2):
            s = lax.dot_general(q2[comp], k_ref[cols, head_lanes(h)], (((1,), (1,)), ((), ())),
                                preferred_element_type=F32)
            s_ref[comp_rows[comp], cols] = s
            mxc = mx[comp]
            for j in range(key_chunk // LANES):
                mxc = jnp.maximum(mxc, s[:, LANES * j:LANES * (j + 1)])
            out.append(mxc)
        return out

    def value_chunk(h, s_ref, c, m, acc):
        cols = slice(key_chunk * c, key_chunk * (c + 1))
        out = []
        for comp in range(2):
            p = jnp.exp(s_ref[comp_rows[comp], cols] - m[comp_rows[comp], :]).astype(BF16)
            out.append(acc[comp] + jnp.dot(p, vext[h, cols, :], preferred_element_type=F32))
        return out

    def finish_tile(h, t, acc):
        o = [a[:, :B_V_DIM] / a[:, B_V_DIM:B_V_DIM + 1] for a in acc]
        diff = o[0] - lam * o[1]
        o_ref[tile_rows(t), head_lanes(h)] = (_rms(diff, gain) * (1.0 - lambda_init)).astype(BF16)

    def run(scores, values, scores_to_odd):
        s_w, m_w = (s_odd, m_odd) if scores_to_odd else (s_even, m_even)
        s_r, m_r = (s_even, m_even) if scores_to_odd else (s_odd, m_odd)
        if scores is not None:
            q2 = stacked_q(*scores)
            mx = [jnp.full((B_BLOCK_Q, LANES), -jnp.inf, F32)] * 2
        if values is not None:
            m = m_r[...]
            acc = [jnp.zeros((B_BLOCK_Q, 2 * B_V_DIM), F32)] * 2
        for c in range(n_chunks):
            if values is not None:
                acc = value_chunk(values[0], s_r, c, m, acc)
            if scores is not None:
                mx = score_chunk(scores[0], q2, s_w, c, mx)
        if scores is not None:
            for comp in range(2):
                m_w[comp_rows[comp], :] = jnp.max(mx[comp], axis=-1, keepdims=True)
        if values is not None:
            finish_tile(*values, acc)

    run((0, 0), None, False)
    for h in range(B_HEADS_PER_STEP):
        def body(i, carry, h=h):
            run((h, 2 * i + 1), (h, 2 * i), True)
            run((h, 2 * i + 2), (h, 2 * i + 1), False)
            return carry

        lax.fori_loop(0, n_tiles // 2 - 1, body, 0)
        run((h, n_tiles - 1), (h, n_tiles - 2), True)
        next_head = (h + 1, 0) if h + 1 < B_HEADS_PER_STEP else None
        run(next_head, (h, n_tiles - 1), False)


def _attn_b(bq, bk, bv, g_out, lq1, lk1, lq2, lk2, batch, seq, lambda_init):
    n = bq.shape[0]
    blk = pl.BlockSpec((seq, LANES * B_HEADS_PER_STEP), lambda b, h: (b, h))
    vec = lambda w: pl.BlockSpec((1, w), lambda b, h: (0, 0))
    return pl.pallas_call(
        functools.partial(_attn_b_kernel, seq=seq, lambda_init=lambda_init),
        grid=(batch, B_HEADS // B_HEADS_PER_STEP),
        in_specs=[blk, blk, blk, vec(B_V_DIM),
                  vec(HEAD_DIM), vec(HEAD_DIM), vec(HEAD_DIM), vec(HEAD_DIM)],
        out_specs=blk,
        out_shape=jax.ShapeDtypeStruct((n, B_WIDTH), BF16),
        scratch_shapes=[pltpu.VMEM((B_HEADS_PER_STEP, seq, 2 * B_V_DIM), BF16),
                        pltpu.VMEM((2 * B_BLOCK_Q, seq), F32),
                        pltpu.VMEM((2 * B_BLOCK_Q, seq), F32),
                        pltpu.VMEM((2 * B_BLOCK_Q, 1), F32),
                        pltpu.VMEM((2 * B_BLOCK_Q, 1), F32)],
        compiler_params=pltpu.CompilerParams(
            dimension_semantics=("parallel", "parallel"), vmem_limit_bytes=VMEM_LIMIT),
        name="attn_b",
    )(bq, bk, bv, g_out, lq1, lk1, lq2, lk2)


def _out_ffn_kernel(x_ref, oa_ref, ob_ref, wo_ref, g2_ref, wgu_ref, wd_ref, gf_ref, o_ref,
                    act_ref):
    x2 = (x_ref[...]
          + jnp.dot(oa_ref[...], wo_ref[:A_WIDTH, :], preferred_element_type=F32)
          + jnp.dot(ob_ref[...], wo_ref[A_WIDTH:, :], preferred_element_type=F32))
    hb = _rms(x2, g2_ref[...]).astype(BF16)
    x3 = x2 + 0.5 * _swiglu(hb, wgu_ref, wd_ref, act_ref)
    o_ref[...] = _rms(x3, gf_ref[...])


def _out_ffn(x1, oa, ob, w_out, g2, wgu, wd, gf):
    n = x1.shape[0]
    row = lambda w: pl.BlockSpec((ROW_TILE, w), lambda i: (i, 0))
    return pl.pallas_call(
        _out_ffn_kernel,
        grid=(n // ROW_TILE,),
        in_specs=[
            row(D_MODEL), row(A_WIDTH), row(B_WIDTH),
            _const_spec((A_WIDTH + B_WIDTH, D_MODEL)),
            _const_spec((1, D_MODEL)),
            _const_spec((D_MODEL, 2 * D_FF)),
            _const_spec((D_FF, D_MODEL)),
            _const_spec((1, D_MODEL)),
        ],
        out_specs=row(D_MODEL),
        out_shape=jax.ShapeDtypeStruct((n, D_MODEL), F32),
        scratch_shapes=[pltpu.VMEM((ROW_TILE, D_FF), BF16)],
        compiler_params=pltpu.CompilerParams(
            dimension_semantics=("parallel",), vmem_limit_bytes=VMEM_LIMIT),
        name="out_ffn",
    )(x1, oa, ob, w_out, g2, wgu, wd, gf)


def _rope_tables(seq):
    inv = 1.0 / (ROPE_THETA ** (jnp.arange(0, HEAD_DIM, 2, dtype=F32) / HEAD_DIM))
    ang = jnp.arange(seq, dtype=F32)[:, None] * inv[None, :]
    ang = jnp.concatenate([ang, ang], axis=-1)
    sign = jnp.where(jnp.arange(HEAD_DIM) < HEAD_DIM // 2, -1.0, 1.0).astype(F32)
    cos = jnp.tile(jnp.cos(ang), (1, LANES // HEAD_DIM))
    sin = jnp.tile(jnp.sin(ang) * sign, (1, LANES // HEAD_DIM))
    return cos, sin


def kernel(x_prompt, x_sample, g_ffn1, w_ffn1_gu, w_ffn1_down, g_mix, w_in, g_a_q, g_a_k, g_a_out, g_b_q, g_b_k, lam_q1, lam_k1, lam_q2, lam_k2, g_b_out, w_out, g_ffn2, w_ffn2_gu, w_ffn2_down, g_final):
    depth = g_ffn1.shape[0]
    scale = HEAD_DIM ** -0.5
    head_mean = (jnp.kron(jnp.eye(MXU_WIDTH // HEAD_DIM, dtype=F32),
                          jnp.ones((HEAD_DIM, HEAD_DIM), F32)) / HEAD_DIM).astype(BF16)
    row = lambda v: v.reshape(1, -1).astype(F32)
    tile_heads = lambda v: jnp.tile(v.astype(F32), A_WIDTH // HEAD_DIM).reshape(1, A_WIDTH)

    layers = []
    for l in range(depth):
        layers.append(dict(
            g1=row(g_ffn1[l]), wgu1=w_ffn1_gu[l].astype(BF16), wd1=w_ffn1_down[l].astype(BF16),
            gmix=row(g_mix[l]), w_in=w_in[l].astype(BF16),
            gaq=tile_heads(g_a_q[l]) * scale, gak=tile_heads(g_a_k[l]),
            gbq=tile_heads(g_b_q[l]) * scale, gbk=tile_heads(g_b_k[l]),
            ga_out=g_a_out[l].astype(F32).reshape(A_WIDTH // LANES, 1, LANES),
            gb_out=row(g_b_out[l]),
            lq1=row(lam_q1[l]), lk1=row(lam_k1[l]), lq2=row(lam_q2[l]), lk2=row(lam_k2[l]),
            w_out=w_out[l].astype(BF16),
            g2=row(g_ffn2[l]), wgu2=w_ffn2_gu[l].astype(BF16), wd2=w_ffn2_down[l].astype(BF16),
            gf=row(g_final[l]),
            lambda_init=0.8 - 0.6 * math.exp(-0.3 * l),
        ))

    def trunk(x):
        batch, seq, _ = x.shape
        cos, sin = _rope_tables(seq)
        x = x.reshape(batch * seq, D_MODEL)
        for p in layers:
            x1 = _ffn(x, p["g1"], p["wgu1"], p["wd1"])
            aq, ak, av, bq, bk, bv = _in_proj(x1, p["gmix"], p["w_in"], head_mean, p["gaq"],
                                              p["gak"], p["gbq"], p["gbk"], cos, sin, seq)
            oa = _attn_a(aq, ak, av, p["ga_out"], batch, seq)
            ob = _attn_b(bq, bk, bv, p["gb_out"], p["lq1"], p["lk1"], p["lq2"], p["lk2"],
                         batch, seq, p["lambda_init"])
            x = _out_ffn(x1, oa, ob, p["w_out"], p["g2"], p["wgu2"], p["wd2"], p["gf"])
        return x.reshape(batch, seq, D_MODEL)

    return (trunk(x_prompt), trunk(x_sample))
```

```python
import functools
import math

import jax
import jax.numpy as jnp
from jax import lax
from jax.experimental import pallas as pl
from jax.experimental.pallas import tpu as pltpu

F32 = jnp.float32
BF16 = jnp.bfloat16

D_MODEL = 1024
HEAD_DIM = 64
A_HEADS = 8
A_WIDTH = A_HEADS * HEAD_DIM
B_HEADS = 4
B_V_DIM = 2 * HEAD_DIM
B_WIDTH = B_HEADS * B_V_DIM
D_FF = 2816
ROPE_THETA = 10000.0
EPS = 1e-6
NEG_INF = -1e30
DILATIONS = (1, 4, 16)
STATE_GROUP = 4
HALF_WINDOW = 64

LANES = 128
MXU_WIDTH = 256
FF_CHUNK = 256
N_FF_CHUNKS = D_FF // FF_CHUNK
ROW_TILE = 1024
A_BLOCK_Q = 128
A_BLOCK_K = A_BLOCK_Q + 2 * HALF_WINDOW
A_UNROLL = 32
B_BLOCK_Q = 256
B_KEY_CHUNKS = 2
B_HEADS_PER_STEP = 1
VMEM_LIMIT = 56 * 1024 * 1024


def _rms(x, g):
    ms = jnp.mean(x * x, axis=-1, keepdims=True)
    return x * lax.rsqrt(ms + EPS) * g


def _swiglu(hb, wgu_ref, wd_ref, act_ref):
    for c in range(N_FF_CHUNKS):
        cols = slice(FF_CHUNK * c, FF_CHUNK * (c + 1))
        g = jnp.dot(hb, wgu_ref[:, cols], preferred_element_type=F32)
        u = jnp.dot(hb, wgu_ref[:, D_FF + FF_CHUNK * c:D_FF + FF_CHUNK * (c + 1)],
                    preferred_element_type=F32)
        act_ref[:, cols] = (g * jax.nn.sigmoid(g) * u).astype(BF16)
    return jnp.dot(act_ref[...], wd_ref[...], preferred_element_type=F32)


def _const_spec(shape):
    return pl.BlockSpec(shape, lambda *_: (0,) * len(shape), pipeline_mode=pl.Buffered(1))


def _ffn_kernel(x_ref, g_ref, wgu_ref, wd_ref, o_ref, act_ref):
    x = x_ref[...]
    hb = _rms(x, g_ref[...]).astype(BF16)
    o_ref[...] = x + 0.5 * _swiglu(hb, wgu_ref, wd_ref, act_ref)


def _ffn(x, g, wgu, wd):
    n = x.shape[0]
    return pl.pallas_call(
        _ffn_kernel,
        grid=(n // ROW_TILE,),
        in_specs=[
            pl.BlockSpec((ROW_TILE, D_MODEL), lambda i: (i, 0)),
            _const_spec((1, D_MODEL)),
            _const_spec((D_MODEL, 2 * D_FF)),
            _const_spec((D_FF, D_MODEL)),
        ],
        out_specs=pl.BlockSpec((ROW_TILE, D_MODEL), lambda i: (i, 0)),
        out_shape=jax.ShapeDtypeStruct((n, D_MODEL), F32),
        scratch_shapes=[pltpu.VMEM((ROW_TILE, D_FF), BF16)],
        compiler_params=pltpu.CompilerParams(
            dimension_semantics=("parallel",), vmem_limit_bytes=VMEM_LIMIT),
        name="ffn",
    )(x, g, wgu, wd)


def _in_proj_kernel(x_ref, g_ref, w_ref, hm_ref, gaq_ref, gak_ref, gbq_ref, gbk_ref,
                    cos_ref, sin_ref, aq_ref, ak_ref, av_ref, bq_ref, bk_ref, bv_ref):
    hb = _rms(x_ref[...], g_ref[...]).astype(BF16)
    cos = cos_ref[...]
    sin = sin_ref[...]
    lane = lax.broadcasted_iota(jnp.int32, cos.shape, 1)
    low_half = (lane & (HEAD_DIM - 1)) < (HEAD_DIM // 2)

    def seg(j):
        return jnp.dot(hb, w_ref[:, A_WIDTH * j:A_WIDTH * (j + 1)], preferred_element_type=F32)

    def norm_rope(j, gain_ref, out_ref):
        p = seg(j)
        sq = (p * p).astype(BF16)
        ms = jnp.concatenate(
            [jnp.dot(sq[:, MXU_WIDTH * c:MXU_WIDTH * (c + 1)], hm_ref[...],
                     preferred_element_type=F32) for c in range(A_WIDTH // MXU_WIDTH)], axis=1)
        y = p * lax.rsqrt(ms + EPS) * gain_ref[...]
        for c in range(A_WIDTH // LANES):
            yc = y[:, LANES * c:LANES * (c + 1)]
            rot = jnp.where(low_half, pltpu.roll(yc, LANES - HEAD_DIM // 2, 1),
                            pltpu.roll(yc, HEAD_DIM // 2, 1))
            out_ref[:, LANES * c:LANES * (c + 1)] = (yc * cos + rot * sin).astype(BF16)

    norm_rope(0, gaq_ref, aq_ref)
    norm_rope(1, gak_ref, ak_ref)
    av_ref[...] = seg(2).astype(BF16)
    norm_rope(3, gbq_ref, bq_ref)
    norm_rope(4, gbk_ref, bk_ref)
    bv_ref[...] = seg(5).astype(BF16)


def _in_proj(x1, g, w_in, head_mean, gaq, gak, gbq, gbk, cos, sin, seq):
    n = x1.shape[0]
    tiles_per_seq = seq // ROW_TILE
    row = pl.BlockSpec((ROW_TILE, A_WIDTH), lambda i: (i, 0))
    table = pl.BlockSpec((ROW_TILE, LANES), lambda i: (i % tiles_per_seq, 0))
    gain = _const_spec((1, A_WIDTH))
    out = jax.ShapeDtypeStruct((n, A_WIDTH), BF16)
    return pl.pallas_call(
        _in_proj_kernel,
        grid=(n // ROW_TILE,),
        in_specs=[
            pl.BlockSpec((ROW_TILE, D_MODEL), lambda i: (i, 0)),
            _const_spec((1, D_MODEL)),
            _const_spec((D_MODEL, 6 * A_WIDTH)),
            _const_spec((MXU_WIDTH, MXU_WIDTH)),
            gain, gain, gain, gain, table, table,
        ],
        out_specs=[row] * 6,
        out_shape=[out] * 6,
        compiler_params=pltpu.CompilerParams(
            dimension_semantics=("parallel",), vmem_limit_bytes=VMEM_LIMIT),
        name="in_proj",
    )(x1, g, w_in, head_mean, gaq, gak, gbq, gbk, cos, sin)


def _attn_a_kernel(q_ref, k_ref, v_ref, g_ref, o_ref,
                   qf, kf, vf, qg, kg, vg, qd, kd, vd, bias, m_run, l_run, acc_run, *, seq):
    lane = lax.broadcasted_iota(jnp.int32, (A_BLOCK_Q, LANES), 1)
    head0 = lane < HEAD_DIM
    qi = lax.broadcasted_iota(jnp.int32, (A_BLOCK_Q, A_BLOCK_K), 0)
    kj = lax.broadcasted_iota(jnp.int32, (A_BLOCK_Q, A_BLOCK_K), 1)
    in_band = jnp.abs(kj - HALF_WINDOW - qi) <= HALF_WINDOW
    ones_cols = jnp.ones((A_BLOCK_K, LANES), BF16)
    for variant in range(4):
        ok = in_band
        if variant & 1:
            ok = ok & (kj >= HALF_WINDOW)
        if variant & 2:
            ok = ok & (kj < A_BLOCK_Q + HALF_WINDOW)
        b = jnp.where(ok, 0.0, NEG_INF)
        bias[variant, :A_BLOCK_Q, :] = b
        bias[variant, A_BLOCK_Q:, :] = b

    copy_rows = 512
    for c in range(seq // copy_rows):
        rows = slice(copy_rows * c, copy_rows * (c + 1))
        qf[rows, :] = q_ref[rows, :].astype(F32)
        kf[rows, :] = k_ref[rows, :].astype(F32)
        vf[rows, :] = v_ref[rows, :].astype(F32)

    offset = 0
    q_offset = 0
    layout = {}
    prev = 1
    source = (qf, kf, vf)
    for d in DILATIONS:
        run = seq // d
        padded = run + 2 * HALF_WINDOW
        chunk = min(run, 256)
        step = d // prev
        keep = (qg, kg, vg) if 1 < d < DILATIONS[-1] else None
        zeros = jnp.zeros((HALF_WINDOW, LANES), BF16)
        head0_c = lax.broadcasted_iota(jnp.int32, (chunk, LANES), 1) < HEAD_DIM
        for r in range(d):
            base = offset + r * padded
            kd[base:base + HALF_WINDOW, :] = zeros
            vd[base:base + HALF_WINDOW, :] = zeros
            kd[base + HALF_WINDOW + run:base + padded, :] = zeros
            vd[base + HALF_WINDOW + run:base + padded, :] = zeros
            for c in range(run // chunk):
                dst = base + HALF_WINDOW + c * chunk
                qdst = q_offset + r * run + c * chunk
                if d == 1:
                    src = pl.ds(c * chunk, chunk)
                    qx = q_ref[src, :].astype(F32)
                    kx = k_ref[src, :]
                    vx = v_ref[src, :]
                else:
                    src = pl.ds((r % prev) * (seq // prev) + r // prev + c * chunk * step,
                                chunk, stride=step)
                    qx, kx, vx = (ref[src, :] for ref in source)
                    if keep is not None:
                        rows = slice(r * run + c * chunk, r * run + (c + 1) * chunk)
                        for ref, x in zip(keep, (qx, kx, vx)):
                            ref[rows, :] = x
                    kx = kx.astype(BF16)
                    vx = vx.astype(BF16)
                qd[0, qdst:qdst + chunk, :] = jnp.where(head0_c, qx, 0.0).astype(BF16)
                qd[1, qdst:qdst + chunk, :] = jnp.where(head0_c, 0.0, qx).astype(BF16)
                kd[dst:dst + chunk, :] = kx
                vd[dst:dst + chunk, :] = vx
        if keep is not None:
            source = keep
        prev = d
        layout[d] = (offset, q_offset)
        offset += d * padded
        q_offset += seq

    group_run = seq // STATE_GROUP
    first = (qf, kf, vf)
    state = (m_run, l_run, acc_run)

    for d in DILATIONS:
        offset, q_offset = layout[d]
        blocks_per_run = seq // d // A_BLOCK_Q

        def block(i, carry, d=d, offset=offset, q_offset=q_offset,
                  blocks_per_run=blocks_per_run):
            r = i // blocks_per_run
            tb = i % blocks_per_run
            q_rows = pl.ds(pl.multiple_of(q_offset + i * A_BLOCK_Q, A_BLOCK_Q), A_BLOCK_Q)
            k_start = pl.multiple_of(offset + i * A_BLOCK_Q + r * (2 * HALF_WINDOW), A_BLOCK_Q)
            q2 = jnp.concatenate([qd[0, q_rows, :], qd[1, q_rows, :]], axis=0)
            kw = kd[pl.ds(k_start, A_BLOCK_K), :]
            vw = vd[pl.ds(k_start, A_BLOCK_K), :]
            variant = jnp.where(tb == 0, 1, 0) + jnp.where(tb == blocks_per_run - 1, 2, 0)
            s = lax.dot_general(q2, kw, (((1,), (1,)), ((), ())), preferred_element_type=F32)
            s = s + bias[variant]
            m = jnp.max(s, axis=-1, keepdims=True)
            p = jnp.exp(s - m).astype(BF16)
            pv = jnp.dot(p, jnp.concatenate([vw, ones_cols], axis=1), preferred_element_type=F32)
            m_new = jnp.where(head0, m[:A_BLOCK_Q], m[A_BLOCK_Q:])
            l_new = jnp.where(head0, pv[:A_BLOCK_Q, LANES:], pv[A_BLOCK_Q:, LANES:])
            acc_new = jnp.where(head0, pv[:A_BLOCK_Q, :LANES], pv[A_BLOCK_Q:, :LANES])
            if d == 1:
                idx = pl.ds(pl.multiple_of(i * A_BLOCK_Q, A_BLOCK_Q), A_BLOCK_Q)
                for ref, x in zip(first, (m_new, l_new, acc_new)):
                    ref[idx, :] = x
            else:
                if d == STATE_GROUP:
                    idx = pl.ds(pl.multiple_of(i * A_BLOCK_Q, A_BLOCK_Q), A_BLOCK_Q)
                else:
                    step = d // STATE_GROUP
                    idx = pl.ds((r % STATE_GROUP) * group_run + r // STATE_GROUP
                                + tb * (A_BLOCK_Q * step), A_BLOCK_Q, stride=step)
                m_old = m_run[idx, :]
                m_tot = jnp.maximum(m_old, m_new)
                a_old = jnp.exp(m_old - m_tot)
                a_new = jnp.exp(m_new - m_tot)
                m_run[idx, :] = m_tot
                l_run[idx, :] = a_old * l_run[idx, :] + a_new * l_new
                acc_run[idx, :] = a_old * acc_run[idx, :] + a_new * acc_new
            return carry

        lax.fori_loop(0, seq // A_BLOCK_Q, block, 0, unroll=A_UNROLL)

        if d == 1:
            chunk = 256
            for g in range(STATE_GROUP):
                for c in range(group_run // chunk):
                    src = pl.ds(g + c * chunk * STATE_GROUP, chunk, stride=STATE_GROUP)
                    dst = slice(g * group_run + c * chunk, g * group_run + (c + 1) * chunk)
                    for ref, tmp in zip(state, first):
                        ref[dst, :] = tmp[src, :]

    gain = g_ref[0]
    out_rows = 256
    chunks_per_group = group_run // out_rows
    out32 = qf
    same_head = ((lax.broadcasted_iota(jnp.int32, (LANES, LANES), 0) < HEAD_DIM)
                 == (lax.broadcasted_iota(jnp.int32, (LANES, LANES), 1) < HEAD_DIM))
    head_mean = jnp.where(same_head, 1.0 / HEAD_DIM, 0.0).astype(BF16)

    def finish(c, carry):
        rows = pl.ds(pl.multiple_of(c * out_rows, out_rows), out_rows)
        o = acc_run[rows, :] / l_run[rows, :]
        ms = jnp.dot((o * o).astype(BF16), head_mean, preferred_element_type=F32)
        g = c // chunks_per_group
        t0 = (c % chunks_per_group) * out_rows
        out32[pl.ds(g + t0 * STATE_GROUP, out_rows, stride=STATE_GROUP), :] = (
            o * lax.rsqrt(ms + EPS) * gain)
        return carry

    lax.fori_loop(0, seq // out_rows, finish, 0, unroll=2)
    copy_rows = 512
    for c in range(seq // copy_rows):
        rows = slice(copy_rows * c, copy_rows * (c + 1))
        o_ref[rows, :] = out32[rows, :].astype(BF16)


def _attn_a(aq, ak, av, g_pairs, batch, seq):
    n = aq.shape[0]
    pairs = A_WIDTH // LANES
    blk = pl.BlockSpec((seq, LANES), lambda b, p: (b, p))
    kv_rows = sum(seq + 2 * HALF_WINDOW * d for d in DILATIONS)
    return pl.pallas_call(
        functools.partial(_attn_a_kernel, seq=seq),
        grid=(batch, pairs),
        in_specs=[blk, blk, blk, pl.BlockSpec((1, 1, LANES), lambda b, p: (p, 0, 0))],
        out_specs=blk,
        out_shape=jax.ShapeDtypeStruct((n, A_WIDTH), BF16),
        scratch_shapes=[pltpu.VMEM((seq, LANES), F32)] * 6 + [
            pltpu.VMEM((2, len(DILATIONS) * seq, LANES), BF16),
            pltpu.VMEM((kv_rows, LANES), BF16), pltpu.VMEM((kv_rows, LANES), BF16),
            pltpu.VMEM((4, 2 * A_BLOCK_Q, A_BLOCK_K), F32),
            pltpu.VMEM((seq, LANES), F32), pltpu.VMEM((seq, LANES), F32),
            pltpu.VMEM((seq, LANES), F32),
        ],
        compiler_params=pltpu.CompilerParams(
            dimension_semantics=("parallel", "parallel"), vmem_limit_bytes=VMEM_LIMIT),
        name="attn_a",
    )(aq, ak, av, g_pairs)


def _attn_b_kernel(q_ref, k_ref, v_ref, g_ref, lq1_ref, lk1_ref, lq2_ref, lk2_ref, o_ref,
                   vext, s_even, s_odd, m_even, m_odd, *, seq, lambda_init):
    for h in range(B_HEADS_PER_STEP):
        vext[h, :, :B_V_DIM] = v_ref[:, LANES * h:LANES * (h + 1)]
        vext[h, :, B_V_DIM:] = jnp.ones((seq, B_V_DIM), BF16)

    lane = lax.broadcasted_iota(jnp.int32, (B_BLOCK_Q, LANES), 1)
    comp0 = (lane < HEAD_DIM).astype(F32)
    comp1 = 1.0 - comp0
    lam = (jnp.exp(jnp.sum(lq1_ref[...] * lk1_ref[...], axis=-1, keepdims=True))
           - jnp.exp(jnp.sum(lq2_ref[...] * lk2_ref[...], axis=-1, keepdims=True)) + lambda_init)
    gain = g_ref[...]
    n_chunks = B_KEY_CHUNKS
    key_chunk = seq // n_chunks
    n_tiles = seq // B_BLOCK_Q

    def tile_rows(t):
        return pl.ds(pl.multiple_of(t * B_BLOCK_Q, B_BLOCK_Q), B_BLOCK_Q)

    comp_rows = (slice(0, B_BLOCK_Q), slice(B_BLOCK_Q, 2 * B_BLOCK_Q))

    def head_lanes(h):
        return slice(LANES * h, LANES * (h + 1))

    def stacked_q(h, t):
        q = q_ref[tile_rows(t), head_lanes(h)].astype(F32)
        return ((q * comp0).astype(BF16), (q * comp1).astype(BF16))

    def score_chunk(h, q2, s_ref, c, mx):
        cols = slice(key_chunk * c, key_chunk * (c + 1))
        out = []
        for comp in range(2):
            s = lax.dot_general(q2[comp], k_ref[cols, head_lanes(h)], (((1,), (1,)), ((), ())),
                                preferred_element_type=F32)
            s_ref[comp_rows[comp], cols] = s
            mxc = mx[comp]
            for j in range(key_chunk // LANES):
                mxc = jnp.maximum(mxc, s[:, LANES * j:LANES * (j + 1)])
            out.append(mxc)
        return out

    def value_chunk(h, s_ref, c, m, acc):
        cols = slice(key_chunk * c, key_chunk * (c + 1))
        out = []
        for comp in range(2):
            p = jnp.exp(s_ref[comp_rows[comp], cols] - m[comp_rows[comp], :]).astype(BF16)
            out.append(acc[comp] + jnp.dot(p, vext[h, cols, :], preferred_element_type=F32))
        return out

    def finish_tile(h, t, acc):
        o = [a[:, :B_V_DIM] / a[:, B_V_DIM:B_V_DIM + 1] for a in acc]
        diff = o[0] - lam * o[1]
        o_ref[tile_rows(t), head_lanes(h)] = (_rms(diff, gain) * (1.0 - lambda_init)).astype(BF16)

    def run(scores, values, scores_to_odd):
        s_w, m_w = (s_odd, m_odd) if scores_to_odd else (s_even, m_even)
        s_r, m_r = (s_even, m_even) if scores_to_odd else (s_odd, m_odd)
        if scores is not None:
            q2 = stacked_q(*scores)
            mx = [jnp.full((B_BLOCK_Q, LANES), -jnp.inf, F32)] * 2
        if values is not None:
            m = m_r[...]
            acc = [jnp.zeros((B_BLOCK_Q, 2 * B_V_DIM), F32)] * 2
        for c in range(n_chunks):
            if values is not None:
                acc = value_chunk(values[0], s_r, c, m, acc)
            if scores is not None:
                mx = score_chunk(scores[0], q2, s_w, c, mx)
        if scores is not None:
            for comp in range(2):
                m_w[comp_rows[comp], :] = jnp.max(mx[comp], axis=-1, keepdims=True)
        if values is not None:
            finish_tile(*values, acc)

    run((0, 0), None, False)
    for h in range(B_HEADS_PER_STEP):
        def body(i, carry, h=h):
            run((h, 2 * i + 1), (h, 2 * i), True)
            run((h, 2 * i + 2), (h, 2 * i + 1), False)
            return carry

        lax.fori_loop(0, n_tiles // 2 - 1, body, 0)
        run((h, n_tiles - 1), (h, n_tiles - 2), True)
        next_head = (h + 1, 0) if h + 1 < B_HEADS_PER_STEP else None
        run(next_head, (h, n_tiles - 1), False)


def _attn_b(bq, bk, bv, g_out, lq1, lk1, lq2, lk2, batch, seq, lambda_init):
    n = bq.shape[0]
    blk = pl.BlockSpec((seq, LANES * B_HEADS_PER_STEP), lambda b, h: (b, h))
    vec = lambda w: pl.BlockSpec((1, w), lambda b, h: (0, 0))
    return pl.pallas_call(
        functools.partial(_attn_b_kernel, seq=seq, lambda_init=lambda_init),
        grid=(batch, B_HEADS // B_HEADS_PER_STEP),
        in_specs=[blk, blk, blk, vec(B_V_DIM),
                  vec(HEAD_DIM), vec(HEAD_DIM), vec(HEAD_DIM), vec(HEAD_DIM)],
        out_specs=blk,
        out_shape=jax.ShapeDtypeStruct((n, B_WIDTH), BF16),
        scratch_shapes=[pltpu.VMEM((B_HEADS_PER_STEP, seq, 2 * B_V_DIM), BF16),
                        pltpu.VMEM((2 * B_BLOCK_Q, seq), F32),
                        pltpu.VMEM((2 * B_BLOCK_Q, seq), F32),
                        pltpu.VMEM((2 * B_BLOCK_Q, 1), F32),
                        pltpu.VMEM((2 * B_BLOCK_Q, 1), F32)],
        compiler_params=pltpu.CompilerParams(
            dimension_semantics=("parallel", "parallel"), vmem_limit_bytes=VMEM_LIMIT),
        name="attn_b",
    )(bq, bk, bv, g_out, lq1, lk1, lq2, lk2)


def _out_ffn_kernel(x_ref, oa_ref, ob_ref, wo_ref, g2_ref, wgu_ref, wd_ref, gf_ref, o_ref,
                    act_ref):
    x2 = (x_ref[...]
          + jnp.dot(oa_ref[...], wo_ref[:A_WIDTH, :], preferred_element_type=F32)
          + jnp.dot(ob_ref[...], wo_ref[A_WIDTH:, :], preferred_element_type=F32))
    hb = _rms(x2, g2_ref[...]).astype(BF16)
    x3 = x2 + 0.5 * _swiglu(hb, wgu_ref, wd_ref, act_ref)
    o_ref[...] = _rms(x3, gf_ref[...])


def _out_ffn(x1, oa, ob, w_out, g2, wgu, wd, gf):
    n = x1.shape[0]
    row = lambda w: pl.BlockSpec((ROW_TILE, w), lambda i: (i, 0))
    return pl.pallas_call(
        _out_ffn_kernel,
        grid=(n // ROW_TILE,),
        in_specs=[
            row(D_MODEL), row(A_WIDTH), row(B_WIDTH),
            _const_spec((A_WIDTH + B_WIDTH, D_MODEL)),
            _const_spec((1, D_MODEL)),
            _const_spec((D_MODEL, 2 * D_FF)),
            _const_spec((D_FF, D_MODEL)),
            _const_spec((1, D_MODEL)),
        ],
        out_specs=row(D_MODEL),
        out_shape=jax.ShapeDtypeStruct((n, D_MODEL), F32),
        scratch_shapes=[pltpu.VMEM((ROW_TILE, D_FF), BF16)],
        compiler_params=pltpu.CompilerParams(
            dimension_semantics=("parallel",), vmem_limit_bytes=VMEM_LIMIT),
        name="out_ffn",
    )(x1, oa, ob, w_out, g2, wgu, wd, gf)


def _rope_tables(seq):
    inv = 1.0 / (ROPE_THETA ** (jnp.arange(0, HEAD_DIM, 2, dtype=F32) / HEAD_DIM))
    ang = jnp.arange(seq, dtype=F32)[:, None] * inv[None, :]
    ang = jnp.concatenate([ang, ang], axis=-1)
    sign = jnp.where(jnp.arange(HEAD_DIM) < HEAD_DIM // 2, -1.0, 1.0).astype(F32)
    cos = jnp.tile(jnp.cos(ang), (1, LANES // HEAD_DIM))
    sin = jnp.tile(jnp.sin(ang) * sign, (1, LANES // HEAD_DIM))
    return cos, sin


def kernel(x_prompt, x_sample, g_ffn1, w_ffn1_gu, w_ffn1_down, g_mix, w_in, g_a_q, g_a_k, g_a_out, g_b_q, g_b_k, lam_q1, lam_k1, lam_q2, lam_k2, g_b_out, w_out, g_ffn2, w_ffn2_gu, w_ffn2_down, g_final):
    depth = g_ffn1.shape[0]
    scale = HEAD_DIM ** -0.5
    head_mean = (jnp.kron(jnp.eye(MXU_WIDTH // HEAD_DIM, dtype=F32),
                          jnp.ones((HEAD_DIM, HEAD_DIM), F32)) / HEAD_DIM).astype(BF16)
    row = lambda v: v.reshape(1, -1).astype(F32)
    tile_heads = lambda v: jnp.tile(v.astype(F32), A_WIDTH // HEAD_DIM).reshape(1, A_WIDTH)

    layers = []
    for l in range(depth):
        layers.append(dict(
            g1=row(g_ffn1[l]), wgu1=w_ffn1_gu[l].astype(BF16), wd1=w_ffn1_down[l].astype(BF16),
            gmix=row(g_mix[l]), w_in=w_in[l].astype(BF16),
            gaq=tile_heads(g_a_q[l]) * scale, gak=tile_heads(g_a_k[l]),
            gbq=tile_heads(g_b_q[l]) * scale, gbk=tile_heads(g_b_k[l]),
            ga_out=g_a_out[l].astype(F32).reshape(A_WIDTH // LANES, 1, LANES),
            gb_out=row(g_b_out[l]),
            lq1=row(lam_q1[l]), lk1=row(lam_k1[l]), lq2=row(lam_q2[l]), lk2=row(lam_k2[l]),
            w_out=w_out[l].astype(BF16),
            g2=row(g_ffn2[l]), wgu2=w_ffn2_gu[l].astype(BF16), wd2=w_ffn2_down[l].astype(BF16),
            gf=row(g_final[l]),
            lambda_init=0.8 - 0.6 * math.exp(-0.3 * l),
        ))

    def trunk(x):
        batch, seq, _ = x.shape
        cos, sin = _rope_tables(seq)
        x = x.reshape(batch * seq, D_MODEL)
        for p in layers:
            x1 = _ffn(x, p["g1"], p["wgu1"], p["wd1"])
            aq, ak, av, bq, bk, bv = _in_proj(x1, p["gmix"], p["w_in"], head_mean, p["gaq"],
                                              p["gak"], p["gbq"], p["gbk"], cos, sin, seq)
            oa = _attn_a(aq, ak, av, p["ga_out"], batch, seq)
            ob = _attn_b(bq, bk, bv, p["gb_out"], p["lq1"], p["lk1"], p["lq2"], p["lk2"],
                         batch, seq, p["lambda_init"])
            x = _out_ffn(x1, oa, ob, p["w_out"], p["g2"], p["wgu2"], p["wd2"], p["gf"])
        return x.reshape(batch, seq, D_MODEL)

    return (trunk(x_prompt), trunk(x_sample))
```
